```python
import math
import jax, jax.numpy as jnp
from jax import lax
import numpy as np


D_MODEL = 1024
BATCH = 4
SEQ = 4096
DEPTH = 4
DEC_BATCH = 128
DEC_SEQ = 8
PAST_LEN = 8192
PAGE_SIZE = 128

N_A = DEPTH // 2
N_B = DEPTH - N_A
D_RNN = D_MODEL
LRU_BW = 256
LRU_BLOCKS = D_RNN // LRU_BW
CONV_W = 4
LRU_C = 8.0
HEAD_DIM = 64
N_HEADS = D_MODEL // HEAD_DIM
N_KV = max(1, N_HEADS // 8)
GROUP = N_HEADS // N_KV
WINDOW = 128
D_FF = ((8 * D_MODEL // 3 + 127) // 128) * 128
EPS = 1e-6
NEG = -1e30

kernel_name = 'yoco_hawk_swa_sink_alibi_decoder'


def rms_norm(x, g):
    xf = x.astype(jnp.float32)
    y = xf * lax.rsqrt(jnp.mean(xf * xf, axis=-1, keepdims=True) + EPS)
    return (y * g.astype(jnp.float32)).astype(x.dtype)


def swiglu(x, w_gu, w_down):
    gu = x @ w_gu
    return (jax.nn.silu(gu[..., :D_FF]) * gu[..., D_FF:]) @ w_down


def alibi_slopes():
    return jnp.exp2(-8.0 * jnp.arange(1, N_HEADS + 1, dtype=jnp.float32) / N_HEADS)


def causal_conv(x, prefix, w, b):
    xe = jnp.concatenate([prefix.astype(x.dtype), x], axis=1)
    t = x.shape[1]
    y = b + sum(xe[:, k:k + t] * w[k] for k in range(CONV_W))
    return y, xe[:, xe.shape[1] - (CONV_W - 1):]


def block_diag(x, w, b):
    xb = x.reshape(x.shape[:-1] + (LRU_BLOCKS, LRU_BW))
    y = jnp.einsum('btnc,ncd->btnd', xb, w.astype(jnp.float32))
    return y.reshape(x.shape) + b.astype(jnp.float32)


def rg_lru(x, h0, w_a, b_a, w_x, b_x, lam):
    xf = x.astype(jnp.float32)
    r = jax.nn.sigmoid(block_diag(xf, w_a, b_a))
    i = jax.nn.sigmoid(block_diag(xf, w_x, b_x))
    log_a = -LRU_C * r * jax.nn.softplus(-lam.astype(jnp.float32))
    a = jnp.exp(log_a)
    u = jnp.sqrt(-jnp.expm1(2.0 * log_a)) * (i * xf)
    u = u.at[:, 0].add(a[:, 0] * h0.astype(jnp.float32))

    def combine(left, right):
        al, bl = left
        ar, br = right
        return al * ar, ar * bl + br

    _, h = lax.associative_scan(combine, (a, u), axis=1)
    return h, h[:, -1]


def recurrent_block(x, h0, buf0, w_in, conv_w, conv_b, w_a, b_a, w_x, b_x, lam, w_out):
    proj = x @ w_in
    gate, xr = proj[..., :D_RNN], proj[..., D_RNN:]
    xc, buf = causal_conv(xr, buf0, conv_w, conv_b)
    h, h_last = rg_lru(xc, h0, w_a, b_a, w_x, b_x, lam)
    y = (h.astype(x.dtype) * jax.nn.gelu(gate)) @ w_out
    return y, h_last.astype(x.dtype), buf


def window_attention(q, k, v, q_pos, k_pos, sinks):
    b, n, tq = q.shape[:3]
    qg = q.reshape(b, n, tq, N_KV, GROUP, HEAD_DIM)
    s = jnp.einsum('bnqkgd,bnskd->bnkgqs', qg, k).astype(jnp.float32) * (HEAD_DIM ** -0.5)
    diff = q_pos[:, :, None] - k_pos[:, None, :]
    valid = (diff >= 0) & (diff < WINDOW) & (k_pos[:, None, :] >= 0)
    slopes = alibi_slopes().reshape(N_KV, GROUP)
    bias = -slopes[None, :, :, None, None] * diff.astype(jnp.float32)[:, None, None]
    s = jnp.where(valid[:, None, None], s + bias, NEG)
    sink = sinks.astype(jnp.float32).reshape(N_KV, GROUP)[:, :, None, None]
    m = jnp.maximum(jnp.max(s, axis=-1, keepdims=True), sink)
    p = jnp.exp(s - m)
    p = p / (jnp.sum(p, axis=-1, keepdims=True) + jnp.exp(sink - m))
    o = jnp.einsum('bnkgqs,bnskd->bnqkgd', p.astype(v.dtype), v)
    return o.reshape(b, n, tq, N_HEADS * HEAD_DIM)


def prompt_window_attention(q, k, v, sinks):
    b, t = q.shape[:2]
    nb = t // WINDOW
    qb = q.reshape(b, nb, WINDOW, N_HEADS, HEAD_DIM)

    def band(z):
        zb = z.reshape(b, nb, WINDOW, N_KV, HEAD_DIM)
        prev = jnp.concatenate([jnp.zeros_like(zb[:, :1]), zb[:, :-1]], axis=1)
        return jnp.concatenate([prev, zb], axis=2)

    start = jnp.arange(nb, dtype=jnp.int32)[:, None] * WINDOW
    q_pos = start + jnp.arange(WINDOW, dtype=jnp.int32)[None]
    k_pos = start - WINDOW + jnp.arange(2 * WINDOW, dtype=jnp.int32)[None]
    o = window_attention(qb, band(k), band(v), q_pos, k_pos, sinks)
    return o.reshape(b, t, N_HEADS * HEAD_DIM)


def sample_window_attention(q, k_all, v_all, sinks):
    b, tq = q.shape[:2]
    tk = k_all.shape[1]
    q_pos = PAST_LEN + jnp.arange(tq, dtype=jnp.int32)
    k_pos = PAST_LEN + tq - tk + jnp.arange(tk, dtype=jnp.int32)
    o = window_attention(q.reshape(b, 1, tq, N_HEADS, HEAD_DIM), k_all[:, None], v_all[:, None],
                         q_pos[None], k_pos[None], sinks)
    return o.reshape(b, tq, N_HEADS * HEAD_DIM)


def trunk(x, rg_h0, rg_buf0, cache_k, cache_v, p):
    rg_h, rg_buf = [], []
    k_att = v_att = new_k = new_v = None
    for l in range(DEPTH):
        x = x + 0.5 * swiglu(rms_norm(x, p['ffn1_norm'][l]), p['ffn1_w_gu'][l], p['ffn1_w_down'][l])
        hn = rms_norm(x, p['mix_norm'][l])
        if l < N_A:
            y, hl, bl = recurrent_block(hn, rg_h0[l], rg_buf0[l], p['rg_w_in'][l], p['rg_conv_w'][l],
                                        p['rg_conv_b'][l], p['rg_w_a'][l], p['rg_b_a'][l], p['rg_w_x'][l],
                                        p['rg_b_x'][l], p['rg_lambda'][l], p['rg_w_out'][l])
            rg_h.append(hl)
            rg_buf.append(bl)
        else:
            j = l - N_A
            q = hn @ p['attn_w_q'][j]
            if cache_k is None:
                o = prompt_window_attention(q, k_att, v_att, p['attn_sinks'][j])
            else:
                o = sample_window_attention(q, k_att, v_att, p['attn_sinks'][j])
            y = o @ p['attn_w_o'][j]
        x = x + y
        x = x + 0.5 * swiglu(rms_norm(x, p['ffn2_norm'][l]), p['ffn2_w_gu'][l], p['ffn2_w_down'][l])
        if l == N_A - 1:
            b, t = x.shape[:2]
            kv = rms_norm(x, p['kv_norm']) @ p['w_kv']
            k = kv[..., :N_KV * HEAD_DIM].reshape(b, t, N_KV, HEAD_DIM)
            v = kv[..., N_KV * HEAD_DIM:].reshape(b, t, N_KV, HEAD_DIM)
            if cache_k is None:
                wb = min(WINDOW, t)
                k_att, v_att = k, v
            else:
                wb = cache_k.shape[1]
                k_att = jnp.concatenate([cache_k.astype(k.dtype), k], axis=1)
                v_att = jnp.concatenate([cache_v.astype(v.dtype), v], axis=1)
            new_k = k_att[:, k_att.shape[1] - wb:]
            new_v = v_att[:, v_att.shape[1] - wb:]
    y = rms_norm(x, p['final_norm'])
    return y, jnp.stack(rg_h), jnp.stack(rg_buf), new_k, new_v


def setup_inputs(seed: int = 0) -> dict:
    key = jax.random.key(seed)
    ks = iter(jax.random.split(key, 40))
    f32 = jnp.float32

    def nrm(shape, scale):
        return jax.random.normal(next(ks), shape, f32) * scale

    win_buf = min(WINDOW, PAST_LEN)
    a0 = jax.random.uniform(next(ks), (N_A, D_RNN), f32, 0.9, 0.999)
    s0 = a0 ** (1.0 / LRU_C)
    lam = jnp.log(s0) - jnp.log1p(-s0)
    return {
        'x_prompt': nrm((BATCH, SEQ, D_MODEL), 1.0),
        'x_sample': nrm((DEC_BATCH, DEC_SEQ, D_MODEL), 1.0),
        'state_rg_h': nrm((N_A, DEC_BATCH, D_RNN), 0.5),
        'state_rg_conv': nrm((N_A, DEC_BATCH, CONV_W - 1, D_RNN), 1.0),
        'cache_k': nrm((DEC_BATCH, win_buf, N_KV, HEAD_DIM), 1.0),
        'cache_v': nrm((DEC_BATCH, win_buf, N_KV, HEAD_DIM), 1.0),
        'ffn1_norm': 1.0 + nrm((DEPTH, D_MODEL), 0.01),
        'ffn1_w_gu': nrm((DEPTH, D_MODEL, 2 * D_FF), D_MODEL ** -0.5),
        'ffn1_w_down': nrm((DEPTH, D_FF, D_MODEL), D_FF ** -0.5),
        'mix_norm': 1.0 + nrm((DEPTH, D_MODEL), 0.01),
        'ffn2_norm': 1.0 + nrm((DEPTH, D_MODEL), 0.01),
        'ffn2_w_gu': nrm((DEPTH, D_MODEL, 2 * D_FF), D_MODEL ** -0.5),
        'ffn2_w_down': nrm((DEPTH, D_FF, D_MODEL), D_FF ** -0.5),
        'rg_w_in': nrm((N_A, D_MODEL, 2 * D_RNN), D_MODEL ** -0.5),
        'rg_conv_w': nrm((N_A, CONV_W, D_RNN), CONV_W ** -0.5),
        'rg_conv_b': nrm((N_A, D_RNN), 0.01),
        'rg_w_a': nrm((N_A, LRU_BLOCKS, LRU_BW, LRU_BW), LRU_BW ** -0.5),
        'rg_b_a': nrm((N_A, D_RNN), 0.01),
        'rg_w_x': nrm((N_A, LRU_BLOCKS, LRU_BW, LRU_BW), LRU_BW ** -0.5),
        'rg_b_x': nrm((N_A, D_RNN), 0.01),
        'rg_lambda': lam,
        'rg_w_out': nrm((N_A, D_RNN, D_MODEL), D_RNN ** -0.5),
        'kv_norm': 1.0 + nrm((D_MODEL,), 0.01),
        'w_kv': nrm((D_MODEL, 2 * N_KV * HEAD_DIM), D_MODEL ** -0.5),
        'attn_w_q': nrm((N_B, D_MODEL, N_HEADS * HEAD_DIM), D_MODEL ** -0.5),
        'attn_sinks': nrm((N_B, N_HEADS), 1.0),
        'attn_w_o': nrm((N_B, N_HEADS * HEAD_DIM, D_MODEL), (N_HEADS * HEAD_DIM) ** -0.5),
        'final_norm': 1.0 + nrm((D_MODEL,), 0.01),
    }


def reference(x_prompt, x_sample, state_rg_h, state_rg_conv, cache_k, cache_v,
              ffn1_norm, ffn1_w_gu, ffn1_w_down, mix_norm, ffn2_norm, ffn2_w_gu, ffn2_w_down,
              rg_w_in, rg_conv_w, rg_conv_b, rg_w_a, rg_b_a, rg_w_x, rg_b_x, rg_lambda, rg_w_out,
              kv_norm, w_kv, attn_w_q, attn_sinks, attn_w_o, final_norm):
    p = dict(ffn1_norm=ffn1_norm, ffn1_w_gu=ffn1_w_gu, ffn1_w_down=ffn1_w_down, mix_norm=mix_norm,
             ffn2_norm=ffn2_norm, ffn2_w_gu=ffn2_w_gu, ffn2_w_down=ffn2_w_down,
             rg_w_in=rg_w_in, rg_conv_w=rg_conv_w, rg_conv_b=rg_conv_b, rg_w_a=rg_w_a, rg_b_a=rg_b_a,
             rg_w_x=rg_w_x, rg_b_x=rg_b_x, rg_lambda=rg_lambda, rg_w_out=rg_w_out,
             kv_norm=kv_norm, w_kv=w_kv, attn_w_q=attn_w_q, attn_sinks=attn_sinks, attn_w_o=attn_w_o,
             final_norm=final_norm)
    b = x_prompt.shape[0]
    h0 = jnp.zeros((N_A, b, D_RNN), x_prompt.dtype)
    buf0 = jnp.zeros((N_A, b, CONV_W - 1, D_RNN), x_prompt.dtype)
    y_prompt, p_h, p_conv, p_k, p_v = trunk(x_prompt, h0, buf0, None, None, p)
    y_sample, s_h, s_conv, s_k, s_v = trunk(x_sample, state_rg_h, state_rg_conv, cache_k, cache_v, p)
    return (y_prompt, y_sample, p_h, p_conv, p_k, p_v, s_h, s_conv, s_k, s_v)
```

```python
import functools
import math

import jax
import jax.numpy as jnp
from jax import lax
from jax.experimental import pallas as pl
from jax.experimental.pallas import tpu as pltpu

D = 1024
D_FF = 2816
N_LAYERS = 4
N_REC = 2
LRU_BLOCKS = 4
LRU_BW = D // LRU_BLOCKS
CONV_W = 4
LRU_C = 8.0
HEAD_DIM = 64
N_HEADS = D // HEAD_DIM
N_KV = 2
GROUP = N_HEADS // N_KV
WINDOW = 128
EPS = 1e-6
NEG = -1e30

V7X_LANES = 128
V7X_SUBLANES = 8
V7X_VMEM_BYTES = 64 * 1024 * 1024

FF_CHUNK = 256
N_FF_CHUNKS = D_FF // FF_CHUNK
NK = 2 * WINDOW

BF16 = jnp.bfloat16
F32 = jnp.float32


def _vmem_limit(estimate_bytes):
    return int(min(max(estimate_bytes, 16 * 1024 * 1024), V7X_VMEM_BYTES - 6 * 1024 * 1024))


def _params(estimate_bytes, n_axes):
    return pltpu.CompilerParams(dimension_semantics=("arbitrary",) * n_axes,
                                vmem_limit_bytes=_vmem_limit(estimate_bytes))


def _resident(block_shape, index_map):
    return pl.BlockSpec(block_shape, index_map, pipeline_mode=pl.Buffered(1))


def _rms(x, g):
    ms = jnp.mean(x * x, axis=-1, keepdims=True)
    return x * lax.rsqrt(ms + EPS) * g


def _bdot(a, b):
    return jnp.dot(a, b, preferred_element_type=F32)


def _ffn_kernel(x_ref, g_ref, wgu_ref, wd_ref, fg_ref, o_ref, n_ref, acc_ref, *, final_norm):
    x = x_ref[...]
    n_ref[...] = _rms(x, g_ref[...]).astype(BF16)
    acc_ref[...] = jnp.zeros_like(acc_ref)

    def chunk(j, carry):
        gu = _bdot(n_ref[...], wgu_ref[j])
        gate, up = gu[:, :FF_CHUNK], gu[:, FF_CHUNK:]
        h = (gate * jax.nn.sigmoid(gate) * up).astype(BF16)
        acc_ref[...] += _bdot(h, wd_ref[j])
        return carry

    lax.fori_loop(0, N_FF_CHUNKS, chunk, 0)
    y = x + 0.5 * acc_ref[...]
    if final_norm:
        y = _rms(y, fg_ref[...])
    o_ref[...] = y


def _ffn(x2, norm, wgu, wd, fgain, layer, *, final_norm, tm):
    t = x2.shape[0]
    est = (4 * tm * D * 4 + tm * D * 2 + tm * D * 4 + 3 * D * D_FF * 2
           + 3 * tm * 2 * FF_CHUNK * 4 + 2 * tm * D * 4)
    return pl.pallas_call(
        functools.partial(_ffn_kernel, final_norm=final_norm),
        grid=(t // tm,),
        in_specs=[
            pl.BlockSpec((tm, D), lambda i: (i, 0)),
            pl.BlockSpec((None, 1, D), lambda i: (layer, 0, 0)),
            _resident((None, N_FF_CHUNKS, D, 2 * FF_CHUNK), lambda i: (layer, 0, 0, 0)),
            _resident((None, N_FF_CHUNKS, FF_CHUNK, D), lambda i: (layer, 0, 0, 0)),
            pl.BlockSpec((1, D), lambda i: (0, 0)),
        ],
        out_specs=pl.BlockSpec((tm, D), lambda i: (i, 0)),
        out_shape=jax.ShapeDtypeStruct((t, D), F32),
        scratch_shapes=[pltpu.VMEM((tm, D), BF16), pltpu.VMEM((tm, D), F32)],
        compiler_params=_params(est, 1),
        name="ffn",
    )(x2, norm, wgu, wd, fgain)


def _softplus(z):
    return jnp.maximum(z, 0.0) + jnp.log1p(jnp.exp(-jnp.abs(z)))


def _gelu_tanh(z):
    return 0.5 * z * (1.0 + jnp.tanh(math.sqrt(2.0 / math.pi) * (z + 0.044715 * (z * z * z))))


def _rg_kernel(x_ref, h0_ref, buf0_ref, g_ref, win_ref, cw_ref, cb_ref, wa_ref, ba_ref, wx_ref,
               bx_ref, lam_ref, wout_ref, y_ref, hl_ref, bufo_ref,
               ext_ref, hc_ref, a_ref, u_ref, hs_ref, gpre_ref):
    t = pl.program_id(1)
    bb, tt, _ = x_ref.shape
    rows = bb * tt
    hist = V7X_SUBLANES

    @pl.when(t == 0)
    def _():
        ext_ref[:, hist - (CONV_W - 1):hist, :] = buf0_ref[...]
        hc_ref[...] = h0_ref[...]

    x = x_ref[...].reshape(rows, D)
    hn = _rms(x, g_ref[...]).astype(BF16)
    proj = _bdot(hn, win_ref[...])
    gate, xr = proj[:, :D], proj[:, D:]
    ext_ref[:, hist:hist + tt, :] = xr.reshape(bb, tt, D)

    xc = cb_ref[...].reshape(1, 1, D)
    for k in range(CONV_W):
        lo = hist - (CONV_W - 1) + k
        xc = xc + ext_ref[:, lo:lo + tt, :] * cw_ref[k:k + 1, :].reshape(1, 1, D)
    xc = xc.reshape(rows, D)

    xcb = xc.astype(BF16)
    for n in range(LRU_BLOCKS):
        sl = slice(n * LRU_BW, (n + 1) * LRU_BW)
        gpre_ref[0, :, sl] = _bdot(xcb[:, sl], wa_ref[n])
        gpre_ref[1, :, sl] = _bdot(xcb[:, sl], wx_ref[n])
    r = jax.nn.sigmoid(gpre_ref[0] + ba_ref[...])
    i = jax.nn.sigmoid(gpre_ref[1] + bx_ref[...])
    log_a = (-LRU_C) * r * _softplus(-lam_ref[...])
    a = jnp.exp(log_a)
    u = jnp.sqrt(-jnp.tanh(log_a) * (a * a + 1.0)) * (i * xc)

    a3 = a.reshape(rows // V7X_SUBLANES, V7X_SUBLANES, D)
    u3 = u.reshape(rows // V7X_SUBLANES, V7X_SUBLANES, D)
    tin = lax.broadcasted_iota(jnp.int32, a3.shape, 1)
    for s in (1, 2, 4):
        a_prev = pltpu.roll(a3, s, 1)
        u_prev = pltpu.roll(u3, s, 1)
        keep = tin >= s
        u3 = jnp.where(keep, a3 * u_prev + u3, u3)
        a3 = jnp.where(keep, a3 * a_prev, a3)
    a_ref[...] = a3.reshape(bb, tt, D)
    u_ref[...] = u3.reshape(bb, tt, D)

    carry = hc_ref[...]
    for gi in range(tt // V7X_SUBLANES):
        sl = slice(gi * V7X_SUBLANES, (gi + 1) * V7X_SUBLANES)
        h = a_ref[:, sl, :] * carry + u_ref[:, sl, :]
        hs_ref[:, sl, :] = h
        carry = h[:, V7X_SUBLANES - 1:, :]
    hc_ref[...] = carry

    hseq = hs_ref[...].reshape(rows, D)
    y = _bdot((hseq * _gelu_tanh(gate)).astype(BF16), wout_ref[...])
    y_ref[...] = (x + y).reshape(bb, tt, D)

    hl_ref[...] = carry
    bufo_ref[...] = ext_ref[:, hist + tt - (CONV_W - 1):hist + tt, :]
    ext_ref[:, 0:hist, :] = ext_ref[:, tt:tt + hist, :]


def _rg_block(x3, h0, buf0, p, layer, *, bb, tt):
    b, t, _ = x3.shape
    rows = bb * tt
    est = (4 * rows * D * 4 + (2 * D * D + D * D + 2 * LRU_BLOCKS * LRU_BW * LRU_BW) * 2 * 2
           + 16 * rows * D * 4)
    vec = pl.BlockSpec((None, 1, D), lambda i, j: (layer, 0, 0))
    gates = pl.BlockSpec((None, LRU_BLOCKS, LRU_BW, LRU_BW), lambda i, j: (layer, 0, 0, 0))
    return pl.pallas_call(
        _rg_kernel,
        grid=(b // bb, t // tt),
        in_specs=[
            pl.BlockSpec((bb, tt, D), lambda i, j: (i, j, 0)),
            pl.BlockSpec((bb, 1, D), lambda i, j: (i, 0, 0)),
            pl.BlockSpec((bb, CONV_W - 1, D), lambda i, j: (i, 0, 0)),
            vec,
            _resident((None, D, 2 * D), lambda i, j: (layer, 0, 0)),
            pl.BlockSpec((None, CONV_W, D), lambda i, j: (layer, 0, 0)),
            vec,
            gates, vec, gates, vec, vec,
            _resident((None, D, D), lambda i, j: (layer, 0, 0)),
        ],
        out_specs=[
            pl.BlockSpec((bb, tt, D), lambda i, j: (i, j, 0)),
            pl.BlockSpec((bb, 1, D), lambda i, j: (i, 0, 0)),
            pl.BlockSpec((bb, CONV_W - 1, D), lambda i, j: (i, 0, 0)),
        ],
        out_shape=[
            jax.ShapeDtypeStruct((b, t, D), F32),
            jax.ShapeDtypeStruct((b, 1, D), F32),
            jax.ShapeDtypeStruct((b, CONV_W - 1, D), F32),
        ],
        scratch_shapes=[
            pltpu.VMEM((bb, tt + V7X_SUBLANES, D), F32),
            pltpu.VMEM((bb, 1, D), F32),
            pltpu.VMEM((bb, tt, D), F32),
            pltpu.VMEM((bb, tt, D), F32),
            pltpu.VMEM((bb, tt, D), F32),
            pltpu.VMEM((2, rows, D), F32),
        ],
        compiler_params=_params(est, 2),
        name="rg_block",
    )(x3, h0, buf0, p["mix_norm"], p["rg_w_in"], p["rg_conv_w"], p["rg_conv_b"], p["rg_w_a"],
      p["rg_b_a"], p["rg_w_x"], p["rg_b_x"], p["rg_lambda"], p["rg_w_out"])


def _kv_kernel(x_ref, g_ref, w_ref, o_ref):
    o_ref[...] = _bdot(_rms(x_ref[...], g_ref[...]).astype(BF16), w_ref[...])


def _kv_proj(x2, norm, w, *, tm):
    t = x2.shape[0]
    n = w.shape[1]
    return pl.pallas_call(
        _kv_kernel,
        grid=(t // tm,),
        in_specs=[pl.BlockSpec((tm, D), lambda i: (i, 0)),
                  pl.BlockSpec((1, D), lambda i: (0, 0)),
                  _resident((D, n), lambda i: (0, 0))],
        out_specs=pl.BlockSpec((tm, n), lambda i: (i, 0)),
        out_shape=jax.ShapeDtypeStruct((t, n), F32),
        compiler_params=_params(6 * tm * D * 4, 1),
        name="kv_proj",
    )(x2, norm, w)


def _init_attn_tables(bias_ref, sinkc_ref, sink_ref, att_layer, r):
    shape = bias_ref.shape[1:]
    sh = int(math.log2(r))
    rho = lax.broadcasted_iota(jnp.int32, shape, 0)
    col = lax.broadcasted_iota(jnp.int32, shape, 1)
    e = rho >> (sh + 2)
    m = (rho >> sh) & 3
    qi = rho & (r - 1)
    g = 2 * m + e
    diff = qi - col + WINDOW
    valid = (diff >= 0) & (diff < WINDOW)
    for kv in range(N_KV):
        head = (kv * GROUP + g + 1).astype(F32)
        slope = jnp.exp2(-0.5 * head)
        bias_ref[kv] = jnp.where(valid, -slope * diff.astype(F32), NEG)
        gcol = g[:, :1]
        sc = jnp.zeros((shape[0], 1), F32)
        for gg in range(GROUP):
            sc = jnp.where(gcol == gg, sink_ref[att_layer, kv * GROUP + gg], sc)
        sinkc_ref[kv] = sc


def _attn_unit(q, k2, v2, bias_ref, sinkc_ref, o_view, pen):
    r = q.shape[0]
    half = HEAD_DIM
    low = lax.broadcasted_iota(jnp.int32, (1, V7X_LANES), 1) < half
    k_sw = pltpu.roll(k2, half, 1)
    v_sw = pltpu.roll(v2, half, 1)
    ones = jnp.ones(k2.shape, BF16)
    zero_b = jnp.zeros((), BF16)
    for kv in range(N_KV):
        kdup = jnp.where(low, k2, k_sw) if kv == 0 else jnp.where(low, k_sw, k2)
        v_lo = jnp.where(low, v2 if kv == 0 else v_sw, zero_b)
        v_hi = jnp.where(low, zero_b, v_sw if kv == 0 else v2)
        parts = []
        for e in range(2):
            for m in range(GROUP // 2):
                c = kv * (GROUP // 2) + m
                qc = q[:, c * V7X_LANES:(c + 1) * V7X_LANES]
                parts.append(jnp.where(low, qc, 0.0) if e == 0 else jnp.where(low, 0.0, qc))
        qst = jnp.concatenate(parts, axis=0).astype(BF16)
        s = lax.dot_general(qst, kdup, (((1,), (1,)), ((), ())), preferred_element_type=F32)
        s = s + bias_ref[kv]
        if pen is not None:
            colk = lax.broadcasted_iota(jnp.int32, s.shape, 1)
            s = s + jnp.where(colk < WINDOW, pen, 0.0)
        sink = sinkc_ref[kv]
        mx = jnp.maximum(jnp.max(s, axis=1, keepdims=True), sink)
        p = jnp.exp(s - mx).astype(BF16)
        esink = jnp.exp(sink - mx)
        hr = (GROUP // 2) * r
        o_even = _bdot(p[:hr], jnp.concatenate([v_lo, ones], axis=1))
        o_odd = _bdot(p[hr:], jnp.concatenate([v_hi, ones], axis=1))
        for m in range(GROUP // 2):
            c = kv * (GROUP // 2) + m
            rs = slice(m * r, (m + 1) * r)
            ne = o_even[rs, :V7X_LANES] / (o_even[rs, V7X_LANES:] + esink[m * r:(m + 1) * r])
            no = o_odd[rs, :V7X_LANES] / (o_odd[rs, V7X_LANES:] + esink[hr + m * r:hr + (m + 1) * r])
            o_view[:, c * V7X_LANES:(c + 1) * V7X_LANES] = ne + no


def _attn_prompt_kernel(sink_ref, x_ref, kvc_ref, kvp_ref, g_ref, wq_ref, wo_ref, y_ref,
                        kvcat_ref, o_ref, bias_ref, sinkc_ref, *, att_layer):
    bi = pl.program_id(0)
    ti = pl.program_id(1)
    tq = x_ref.shape[0]

    @pl.when((bi == 0) & (ti == 0))
    def _():
        _init_attn_tables(bias_ref, sinkc_ref, sink_ref, att_layer, WINDOW)

    x = x_ref[...]
    hn = _rms(x, g_ref[...]).astype(BF16)
    q = _bdot(hn, wq_ref[...]) * (HEAD_DIM ** -0.5)
    kvcat_ref[0:WINDOW, :] = kvp_ref[...].astype(BF16)
    kvcat_ref[WINDOW:, :] = kvc_ref[...].astype(BF16)
    pen = jnp.where(ti == 0, NEG, 0.0).astype(F32)
    for rb in range(tq // WINDOW):
        rows = slice(rb * WINDOW, rb * WINDOW + NK)
        _attn_unit(q[rb * WINDOW:(rb + 1) * WINDOW],
                   kvcat_ref[rows, 0:V7X_LANES], kvcat_ref[rows, V7X_LANES:2 * V7X_LANES],
                   bias_ref, sinkc_ref, o_ref.at[rb * WINDOW:(rb + 1) * WINDOW, :],
                   pen if rb == 0 else None)
    y_ref[...] = x + _bdot(o_ref[...].astype(BF16), wo_ref[...])


def _attn_prompt(x3, kv3, p, att_layer, *, tq):
    b, t, _ = x3.shape
    sub = tq // WINDOW
    est = (4 * tq * D * 4 + 4 * D * D * 2 + 2 * (tq + WINDOW) * 2 * V7X_LANES * 4
           + N_KV * GROUP * WINDOW * (NK + V7X_LANES) * 4 + 10 * tq * D * 4)
    layer = N_REC + att_layer
    return pl.pallas_call(
        functools.partial(_attn_prompt_kernel, att_layer=att_layer),
        grid=(b, t // tq),
        in_specs=[
            pl.BlockSpec(memory_space=pltpu.SMEM),
            pl.BlockSpec((None, tq, D), lambda i, j: (i, j, 0)),
            pl.BlockSpec((None, tq, 2 * V7X_LANES), lambda i, j: (i, j, 0)),
            pl.BlockSpec((None, WINDOW, 2 * V7X_LANES),
                         lambda i, j: (i, jnp.maximum(j * sub - 1, 0), 0)),
            pl.BlockSpec((None, 1, D), lambda i, j: (layer, 0, 0)),
            _resident((None, D, D), lambda i, j: (att_layer, 0, 0)),
            _resident((None, D, D), lambda i, j: (att_layer, 0, 0)),
        ],
        out_specs=pl.BlockSpec((None, tq, D), lambda i, j: (i, j, 0)),
        out_shape=jax.ShapeDtypeStruct((b, t, D), F32),
        scratch_shapes=[
            pltpu.VMEM((tq + WINDOW, 2 * V7X_LANES), BF16),
            pltpu.VMEM((tq, D), F32),
            pltpu.VMEM((N_KV, GROUP * WINDOW, NK), F32),
            pltpu.VMEM((N_KV, GROUP * WINDOW, 1), F32),
        ],
        compiler_params=_params(est, 2),
        name="attn_prompt",
    )(p["attn_sinks"], x3, kv3, kv3, p["mix_norm"], p["attn_w_q"], p["attn_w_o"])


def _attn_sample_kernel(sink_ref, x_ref, k_ref, v_ref, g_ref, wq_ref, wo_ref, y_ref,
                        q_ref, o_ref, bias_ref, sinkc_ref, *, att_layer):
    bb, tq, _ = x_ref.shape

    @pl.when(pl.program_id(0) == 0)
    def _():
        _init_attn_tables(bias_ref, sinkc_ref, sink_ref, att_layer, tq)

    x = x_ref[...].reshape(bb * tq, D)
    hn = _rms(x, g_ref[...]).astype(BF16)
    q_ref[...] = (_bdot(hn, wq_ref[...]) * (HEAD_DIM ** -0.5)).reshape(bb, tq, D)

    def one_sequence(i, carry):
        _attn_unit(q_ref[i], k_ref[i], v_ref[i], bias_ref, sinkc_ref, o_ref.at[i], None)
        return carry

    lax.fori_loop(0, bb, one_sequence, 0)
    y = x + _bdot(o_ref[...].reshape(bb * tq, D).astype(BF16), wo_ref[...])
    y_ref[...] = y.reshape(bb, tq, D)


def _attn_sample(x3, k_all, v_all, p, att_layer, *, bb):
    b, tq, _ = x3.shape
    est = 4 * D * D * 2 + 12 * bb * tq * D * 4 + 4 * bb * NK * V7X_LANES * 2
    layer = N_REC + att_layer
    return pl.pallas_call(
        functools.partial(_attn_sample_kernel, att_layer=att_layer),
        grid=(b // bb,),
        in_specs=[
            pl.BlockSpec(memory_space=pltpu.SMEM),
            pl.BlockSpec((bb, tq, D), lambda i: (i, 0, 0)),
            pl.BlockSpec((bb, NK, V7X_LANES), lambda i: (i, 0, 0)),
            pl.BlockSpec((bb, NK, V7X_LANES), lambda i: (i, 0, 0)),
            pl.BlockSpec((None, 1, D), lambda i: (layer, 0, 0)),
            _resident((None, D, D), lambda i: (att_layer, 0, 0)),
            _resident((None, D, D), lambda i: (att_layer, 0, 0)),
        ],
        out_specs=pl.BlockSpec((bb, tq, D), lambda i: (i, 0, 0)),
        out_shape=jax.ShapeDtypeStruct((b, tq, D), F32),
        scratch_shapes=[
            pltpu.VMEM((bb, tq, D), F32),
            pltpu.VMEM((bb, tq, D), F32),
            pltpu.VMEM((N_KV, GROUP * tq, NK), F32),
            pltpu.VMEM((N_KV, GROUP * tq, 1), F32),
        ],
        compiler_params=_params(est, 1),
        name="attn_sample",
    )(p["attn_sinks"], x3, k_all, v_all, p["mix_norm"], p["attn_w_q"], p["attn_w_o"])


def _prepare_params(ffn1_norm, ffn1_w_gu, ffn1_w_down, mix_norm, ffn2_norm, ffn2_w_gu, ffn2_w_down,
                    rg_w_in, rg_conv_w, rg_conv_b, rg_w_a, rg_b_a, rg_w_x, rg_b_x, rg_lambda,
                    rg_w_out, kv_norm, w_kv, attn_w_q, attn_sinks, attn_w_o, final_norm):
    def gu(w):
        w = w.astype(BF16).reshape(N_LAYERS, D, 2, N_FF_CHUNKS, FF_CHUNK)
        return w.transpose(0, 3, 1, 2, 4).reshape(N_LAYERS, N_FF_CHUNKS, D, 2 * FF_CHUNK)

    def down(w):
        return w.astype(BF16).reshape(N_LAYERS, N_FF_CHUNKS, FF_CHUNK, D)

    def vec(v):
        return v.reshape(v.shape[0], 1, D)

    return dict(
        ffn1_norm=vec(ffn1_norm), ffn1_w_gu=gu(ffn1_w_gu), ffn1_w_down=down(ffn1_w_down),
        ffn2_norm=vec(ffn2_norm), ffn2_w_gu=gu(ffn2_w_gu), ffn2_w_down=down(ffn2_w_down),
        mix_norm=vec(mix_norm),
        rg_w_in=rg_w_in.astype(BF16), rg_conv_w=rg_conv_w, rg_conv_b=vec(rg_conv_b),
        rg_w_a=rg_w_a.astype(BF16), rg_b_a=vec(rg_b_a), rg_w_x=rg_w_x.astype(BF16),
        rg_b_x=vec(rg_b_x), rg_lambda=vec(rg_lambda), rg_w_out=rg_w_out.astype(BF16),
        kv_norm=kv_norm.reshape(1, D), w_kv=w_kv.astype(BF16),
        attn_w_q=attn_w_q.astype(BF16), attn_sinks=attn_sinks, attn_w_o=attn_w_o.astype(BF16),
        final_norm=final_norm.reshape(1, D),
    )


def _trunk(x, rg_h0, rg_buf0, cache_k, cache_v, p, *, ffn_tm, rg_bb, rg_tt, att_tile):
    b, t, _ = x.shape
    is_prompt = cache_k is None
    rg_h, rg_buf = [], []
    k_att = v_att = kv3 = new_k = new_v = None
    for l in range(N_LAYERS):
        x = _ffn(x.reshape(b * t, D), p["ffn1_norm"], p["ffn1_w_gu"], p["ffn1_w_down"],
                 p["final_norm"], l, final_norm=False, tm=ffn_tm).reshape(b, t, D)
        if l < N_REC:
            x, hl, bl = _rg_block(x, rg_h0[l].reshape(b, 1, D), rg_buf0[l], p, l, bb=rg_bb, tt=rg_tt)
            rg_h.append(hl.reshape(b, D))
            rg_buf.append(bl)
        elif is_prompt:
            x = _attn_prompt(x, kv3, p, l - N_REC, tq=att_tile)
        else:
            x = _attn_sample(x, k_att, v_att, p, l - N_REC, bb=att_tile)
        x = _ffn(x.reshape(b * t, D), p["ffn2_norm"], p["ffn2_w_gu"], p["ffn2_w_down"],
                 p["final_norm"], l, final_norm=(l == N_LAYERS - 1), tm=ffn_tm).reshape(b, t, D)
        if l == N_REC - 1:
            kv = _kv_proj(x.reshape(b * t, D), p["kv_norm"], p["w_kv"], tm=ffn_tm)
            kv3 = kv.reshape(b, t, 2 * N_KV * HEAD_DIM)
            k_new, v_new = kv3[..., :N_KV * HEAD_DIM], kv3[..., N_KV * HEAD_DIM:]
            if is_prompt:
                wb = min(WINDOW, t)
                new_k, new_v = k_new[:, t - wb:], v_new[:, t - wb:]
            else:
                wb = cache_k.shape[1]
                ck = cache_k.reshape(b, wb, N_KV * HEAD_DIM)
                cv = cache_v.reshape(b, wb, N_KV * HEAD_DIM)
                k_cat = jnp.concatenate([ck, k_new], axis=1)
                v_cat = jnp.concatenate([cv, v_new], axis=1)
                new_k, new_v = k_cat[:, t:], v_cat[:, t:]
                pad = ((0, 0), (0, NK - wb - t), (0, 0))
                k_att = jnp.pad(k_cat.astype(BF16), pad)
                v_att = jnp.pad(v_cat.astype(BF16), pad)
            new_k = new_k.reshape(b, wb, N_KV, HEAD_DIM)
            new_v = new_v.reshape(b, wb, N_KV, HEAD_DIM)
    return x, jnp.stack(rg_h), jnp.stack(rg_buf), new_k, new_v


def kernel(x_prompt, x_sample, state_rg_h, state_rg_conv, cache_k, cache_v, ffn1_norm, ffn1_w_gu, ffn1_w_down, mix_norm, ffn2_norm, ffn2_w_gu, ffn2_w_down, rg_w_in, rg_conv_w, rg_conv_b, rg_w_a, rg_b_a, rg_w_x, rg_b_x, rg_lambda, rg_w_out, kv_norm, w_kv, attn_w_q, attn_sinks, attn_w_o, final_norm):
    p = _prepare_params(ffn1_norm, ffn1_w_gu, ffn1_w_down, mix_norm, ffn2_norm, ffn2_w_gu,
                        ffn2_w_down, rg_w_in, rg_conv_w, rg_conv_b, rg_w_a, rg_b_a, rg_w_x, rg_b_x,
                        rg_lambda, rg_w_out, kv_norm, w_kv, attn_w_q, attn_sinks, attn_w_o,
                        final_norm)
    b = x_prompt.shape[0]
    h0 = jnp.zeros((N_REC, b, D), x_prompt.dtype)
    buf0 = jnp.zeros((N_REC, b, CONV_W - 1, D), x_prompt.dtype)
    y_p, p_h, p_conv, p_k, p_v = _trunk(x_prompt, h0, buf0, None, None, p,
                                        ffn_tm=1024, rg_bb=1, rg_tt=512, att_tile=512)
    y_s, s_h, s_conv, s_k, s_v = _trunk(x_sample, state_rg_h, state_rg_conv, cache_k, cache_v, p,
                                        ffn_tm=1024, rg_bb=32, rg_tt=8, att_tile=32)
    return (y_p, y_s, p_h, p_conv, p_k, p_v, s_h, s_conv, s_k, s_v)
```

```python
import functools
import math

import jax
import jax.numpy as jnp
from jax import lax
from jax.experimental import pallas as pl
from jax.experimental.pallas import tpu as pltpu

D = 1024
D_FF = 2816
N_LAYERS = 4
N_REC = 2
LRU_BLOCKS = 4
LRU_BW = D // LRU_BLOCKS
CONV_W = 4
LRU_C = 8.0
HEAD_DIM = 64
N_HEADS = D // HEAD_DIM
N_KV = 2
GROUP = N_HEADS // N_KV
WINDOW = 128
EPS = 1e-6
NEG = -1e30

V7X_LANES = 128
V7X_SUBLANES = 8
V7X_VMEM_BYTES = 64 * 1024 * 1024

FF_CHUNK = 256
N_FF_CHUNKS = D_FF // FF_CHUNK
NK = 2 * WINDOW

BF16 = jnp.bfloat16
F32 = jnp.float32


def _vmem_limit(estimate_bytes):
    return int(min(max(estimate_bytes, 16 * 1024 * 1024), V7X_VMEM_BYTES - 6 * 1024 * 1024))


def _params(estimate_bytes, n_axes):
    return pltpu.CompilerParams(dimension_semantics=("arbitrary",) * n_axes,
                                vmem_limit_bytes=_vmem_limit(estimate_bytes))


def _resident(block_shape, index_map):
    return pl.BlockSpec(block_shape, index_map, pipeline_mode=pl.Buffered(1))


def _rms(x, g):
    ms = jnp.mean(x * x, axis=-1, keepdims=True)
    return x * lax.rsqrt(ms + EPS) * g


def _bdot(a, b):
    return jnp.dot(a, b, preferred_element_type=F32)


def _cast_gu_kernel(g_ref, u_ref, o_ref):
    o_ref[:, :FF_CHUNK] = g_ref[...].astype(BF16)
    o_ref[:, FF_CHUNK:] = u_ref[...].astype(BF16)


def _cast_gu(w):
    return pl.pallas_call(
        _cast_gu_kernel,
        grid=(N_LAYERS, N_FF_CHUNKS),
        in_specs=[pl.BlockSpec((None, D, FF_CHUNK), lambda l, j: (l, 0, j)),
                  pl.BlockSpec((None, D, FF_CHUNK), lambda l, j: (l, 0, N_FF_CHUNKS + j))],
        out_specs=pl.BlockSpec((None, None, D, 2 * FF_CHUNK), lambda l, j: (l, j, 0, 0)),
        out_shape=jax.ShapeDtypeStruct((N_LAYERS, N_FF_CHUNKS, D, 2 * FF_CHUNK), BF16),
        compiler_params=_params(8 * D * FF_CHUNK * 4, 2),
        name="cast_gu",
    )(w, w)


def _cast_kernel(w_ref, o_ref):
    o_ref[...] = w_ref[...].astype(BF16)


def _cast_down(w):
    rows = D_FF // 2
    out = pl.pallas_call(
        _cast_kernel,
        grid=(N_LAYERS, D_FF // rows),
        in_specs=[pl.BlockSpec((None, rows, D), lambda l, j: (l, j, 0))],
        out_specs=pl.BlockSpec((None, rows, D), lambda l, j: (l, j, 0)),
        out_shape=jax.ShapeDtypeStruct((N_LAYERS, D_FF, D), BF16),
        compiler_params=_params(6 * rows * D * 4, 2),
        name="cast_down",
    )(w)
    return out.reshape(N_LAYERS, N_FF_CHUNKS, FF_CHUNK, D)


def _ffn_kernel(x_ref, g_ref, wgu_ref, wd_ref, fg_ref, o_ref, n_ref, h_ref, *, final_norm):
    n_ref[...] = _rms(x_ref[...], g_ref[...]).astype(BF16)
    for j in range(N_FF_CHUNKS):
        gu = _bdot(n_ref[...], wgu_ref[j])
        gate, up = gu[:, :FF_CHUNK], gu[:, FF_CHUNK:]
        h_ref[j] = (gate * jax.nn.sigmoid(gate) * up).astype(BF16)
    acc = _bdot(h_ref[0], wd_ref[0])
    for j in range(1, N_FF_CHUNKS):
        acc = acc + _bdot(h_ref[j], wd_ref[j])
    y = x_ref[...] + 0.5 * acc
    if final_norm:
        y = _rms(y, fg_ref[...])
    o_ref[...] = y


def _ffn(x2, norm, wgu, wd, fgain, layer, *, final_norm, tm):
    t = x2.shape[0]
    est = (4 * tm * D * 4 + tm * D * 2 + tm * D_FF * 2 + 3 * D * D_FF * 2
           + 3 * tm * 2 * FF_CHUNK * 4 + 2 * tm * D * 4)
    return pl.pallas_call(
        functools.partial(_ffn_kernel, final_norm=final_norm),
        grid=(t // tm,),
        in_specs=[
            pl.BlockSpec((tm, D), lambda i: (i, 0)),
            pl.BlockSpec((None, 1, D), lambda i: (layer, 0, 0)),
            _resident((None, N_FF_CHUNKS, D, 2 * FF_CHUNK), lambda i: (layer, 0, 0, 0)),
            _resident((None, N_FF_CHUNKS, FF_CHUNK, D), lambda i: (layer, 0, 0, 0)),
            pl.BlockSpec((1, D), lambda i: (0, 0)),
        ],
        out_specs=pl.BlockSpec((tm, D), lambda i: (i, 0)),
        out_shape=jax.ShapeDtypeStruct((t, D), F32),
        scratch_shapes=[pltpu.VMEM((tm, D), BF16), pltpu.VMEM((N_FF_CHUNKS, tm, FF_CHUNK), BF16)],
        compiler_params=_params(est, 1),
        name="ffn",
    )(x2, norm, wgu, wd, fgain)


def _softplus(z):
    return jnp.maximum(z, 0.0) + jnp.log1p(jnp.exp(-jnp.abs(z)))


def _gelu_tanh(z):
    return 0.5 * z * (1.0 + jnp.tanh(math.sqrt(2.0 / math.pi) * (z + 0.044715 * (z * z * z))))


def _rg_kernel(x_ref, h0_ref, buf0_ref, g_ref, win_ref, cw_ref, cb_ref, wa_ref, ba_ref, wx_ref,
               bx_ref, lam_ref, wout_ref, y_ref, hl_ref, bufo_ref,
               ext_ref, hc_ref, a_ref, u_ref, hs_ref, gpre_ref,
               win_b, wa_b, wx_b, wout_b):
    t = pl.program_id(1)
    bb, tt, _ = x_ref.shape
    rows = bb * tt
    hist = V7X_SUBLANES

    @pl.when((pl.program_id(0) == 0) & (t == 0))
    def _():
        win_b[...] = win_ref[...].astype(BF16)
        wa_b[...] = wa_ref[...].astype(BF16)
        wx_b[...] = wx_ref[...].astype(BF16)
        wout_b[...] = wout_ref[...].astype(BF16)

    @pl.when(t == 0)
    def _():
        ext_ref[:, hist - (CONV_W - 1):hist, :] = buf0_ref[...]
        hc_ref[...] = h0_ref[...]

    x = x_ref[...].reshape(rows, D)
    hn = _rms(x, g_ref[...]).astype(BF16)
    proj = _bdot(hn, win_b[...])
    gate, xr = proj[:, :D], proj[:, D:]
    ext_ref[:, hist:hist + tt, :] = xr.reshape(bb, tt, D)

    xc = cb_ref[...].reshape(1, 1, D)
    for k in range(CONV_W):
        lo = hist - (CONV_W - 1) + k
        xc = xc + ext_ref[:, lo:lo + tt, :] * cw_ref[k:k + 1, :].reshape(1, 1, D)
    xc = xc.reshape(rows, D)

    xcb = xc.astype(BF16)
    for n in range(LRU_BLOCKS):
        sl = slice(n * LRU_BW, (n + 1) * LRU_BW)
        gpre_ref[0, :, sl] = _bdot(xcb[:, sl], wa_b[n])
        gpre_ref[1, :, sl] = _bdot(xcb[:, sl], wx_b[n])
    r = jax.nn.sigmoid(gpre_ref[0] + ba_ref[...])
    i = jax.nn.sigmoid(gpre_ref[1] + bx_ref[...])
    log_a = (-LRU_C) * r * _softplus(-lam_ref[...])
    a = jnp.exp(log_a)
    u = jnp.sqrt(-jnp.tanh(log_a) * (a * a + 1.0)) * (i * xc)

    a3 = a.reshape(rows // V7X_SUBLANES, V7X_SUBLANES, D)
    u3 = u.reshape(rows // V7X_SUBLANES, V7X_SUBLANES, D)
    tin = lax.broadcasted_iota(jnp.int32, a3.shape, 1)
    for s in (1, 2, 4):
        a_prev = pltpu.roll(a3, s, 1)
        u_prev = pltpu.roll(u3, s, 1)
        keep = tin >= s
        u3 = jnp.where(keep, a3 * u_prev + u3, u3)
        a3 = jnp.where(keep, a3 * a_prev, a3)
    a_ref[...] = a3.reshape(bb, tt, D)
    u_ref[...] = u3.reshape(bb, tt, D)

    carry = hc_ref[...]
    for gi in range(tt // V7X_SUBLANES):
        sl = slice(gi * V7X_SUBLANES, (gi + 1) * V7X_SUBLANES)
        h = a_ref[:, sl, :] * carry + u_ref[:, sl, :]
        hs_ref[:, sl, :] = h
        carry = h[:, V7X_SUBLANES - 1:, :]
    hc_ref[...] = carry

    hseq = hs_ref[...].reshape(rows, D)
    y = _bdot((hseq * _gelu_tanh(gate)).astype(BF16), wout_b[...])
    y_ref[...] = (x + y).reshape(bb, tt, D)

    hl_ref[...] = carry
    bufo_ref[...] = ext_ref[:, hist + tt - (CONV_W - 1):hist + tt, :]
    ext_ref[:, 0:hist, :] = ext_ref[:, tt:tt + hist, :]


def _rg_block(x3, h0, buf0, p, layer, *, bb, tt):
    b, t, _ = x3.shape
    rows = bb * tt
    n_w = 2 * D * D + D * D + 2 * LRU_BLOCKS * LRU_BW * LRU_BW
    est = 4 * rows * D * 4 + n_w * (4 + 2) + 16 * rows * D * 4
    vec = pl.BlockSpec((None, 1, D), lambda i, j: (layer, 0, 0))
    gates = _resident((None, LRU_BLOCKS, LRU_BW, LRU_BW), lambda i, j: (layer, 0, 0, 0))
    return pl.pallas_call(
        _rg_kernel,
        grid=(b // bb, t // tt),
        in_specs=[
            pl.BlockSpec((bb, tt, D), lambda i, j: (i, j, 0)),
            pl.BlockSpec((bb, 1, D), lambda i, j: (i, 0, 0)),
            pl.BlockSpec((bb, CONV_W - 1, D), lambda i, j: (i, 0, 0)),
            vec,
            _resident((None, D, 2 * D), lambda i, j: (layer, 0, 0)),
            pl.BlockSpec((None, CONV_W, D), lambda i, j: (layer, 0, 0)),
            vec,
            gates, vec, gates, vec, vec,
            _resident((None, D, D), lambda i, j: (layer, 0, 0)),
        ],
        out_specs=[
            pl.BlockSpec((bb, tt, D), lambda i, j: (i, j, 0)),
            pl.BlockSpec((bb, 1, D), lambda i, j: (i, 0, 0)),
            pl.BlockSpec((bb, CONV_W - 1, D), lambda i, j: (i, 0, 0)),
        ],
        out_shape=[
            jax.ShapeDtypeStruct((b, t, D), F32),
            jax.ShapeDtypeStruct((b, 1, D), F32),
            jax.ShapeDtypeStruct((b, CONV_W - 1, D), F32),
        ],
        scratch_shapes=[
            pltpu.VMEM((bb, tt + V7X_SUBLANES, D), F32),
            pltpu.VMEM((bb, 1, D), F32),
            pltpu.VMEM((bb, tt, D), F32),
            pltpu.VMEM((bb, tt, D), F32),
            pltpu.VMEM((bb, tt, D), F32),
            pltpu.VMEM((2, rows, D), F32),
            pltpu.VMEM((D, 2 * D), BF16),
            pltpu.VMEM((LRU_BLOCKS, LRU_BW, LRU_BW), BF16),
            pltpu.VMEM((LRU_BLOCKS, LRU_BW, LRU_BW), BF16),
            pltpu.VMEM((D, D), BF16),
        ],
        compiler_params=_params(est, 2),
        name="rg_block",
    )(x3, h0, buf0, p["mix_norm"], p["rg_w_in"], p["rg_conv_w"], p["rg_conv_b"], p["rg_w_a"],
      p["rg_b_a"], p["rg_w_x"], p["rg_b_x"], p["rg_lambda"], p["rg_w_out"])


def _kv_kernel(x_ref, g_ref, w_ref, o_ref, w_b):
    @pl.when(pl.program_id(0) == 0)
    def _():
        w_b[...] = w_ref[...].astype(BF16)

    o_ref[...] = _bdot(_rms(x_ref[...], g_ref[...]).astype(BF16), w_b[...])


def _kv_proj(x2, norm, w, *, tm):
    t = x2.shape[0]
    n = w.shape[1]
    return pl.pallas_call(
        _kv_kernel,
        grid=(t // tm,),
        in_specs=[pl.BlockSpec((tm, D), lambda i: (i, 0)),
                  pl.BlockSpec((1, D), lambda i: (0, 0)),
                  _resident((D, n), lambda i: (0, 0))],
        out_specs=pl.BlockSpec((tm, n), lambda i: (i, 0)),
        out_shape=jax.ShapeDtypeStruct((t, n), F32),
        scratch_shapes=[pltpu.VMEM((D, n), BF16)],
        compiler_params=_params(6 * tm * D * 4, 1),
        name="kv_proj",
    )(x2, norm, w)


def _init_attn_tables(bias_ref, sinkc_ref, sink_ref, att_layer, r):
    shape = bias_ref.shape[1:]
    sh = int(math.log2(r))
    rho = lax.broadcasted_iota(jnp.int32, shape, 0)
    col = lax.broadcasted_iota(jnp.int32, shape, 1)
    e = rho >> (sh + 2)
    m = (rho >> sh) & 3
    qi = rho & (r - 1)
    g = 2 * m + e
    diff = qi - col + WINDOW
    valid = (diff >= 0) & (diff < WINDOW)
    for kv in range(N_KV):
        head = (kv * GROUP + g + 1).astype(F32)
        slope = jnp.exp2(-0.5 * head)
        bias_ref[kv] = jnp.where(valid, -slope * diff.astype(F32), NEG)
        gcol = g[:, :1]
        sc = jnp.zeros((shape[0], 1), F32)
        for gg in range(GROUP):
            sc = jnp.where(gcol == gg, sink_ref[att_layer, kv * GROUP + gg], sc)
        sinkc_ref[kv] = sc


def _attn_unit(q, k2, v2, bias_ref, sinkc_ref, o_view, pen):
    r = q.shape[0]
    half = HEAD_DIM
    low = lax.broadcasted_iota(jnp.int32, (1, V7X_LANES), 1) < half
    k_sw = pltpu.roll(k2, half, 1)
    v_sw = pltpu.roll(v2, half, 1)
    ones = jnp.ones(k2.shape, BF16)
    zero_b = jnp.zeros((), BF16)
    for kv in range(N_KV):
        kdup = jnp.where(low, k2, k_sw) if kv == 0 else jnp.where(low, k_sw, k2)
        v_lo = jnp.where(low, v2 if kv == 0 else v_sw, zero_b)
        v_hi = jnp.where(low, zero_b, v_sw if kv == 0 else v2)
        parts = []
        for e in range(2):
            for m in range(GROUP // 2):
                c = kv * (GROUP // 2) + m
                qc = q[:, c * V7X_LANES:(c + 1) * V7X_LANES]
                parts.append(jnp.where(low, qc, 0.0) if e == 0 else jnp.where(low, 0.0, qc))
        qst = jnp.concatenate(parts, axis=0).astype(BF16)
        s = lax.dot_general(qst, kdup, (((1,), (1,)), ((), ())), preferred_element_type=F32)
        s = s + bias_ref[kv]
        if pen is not None:
            colk = lax.broadcasted_iota(jnp.int32, s.shape, 1)
            s = s + jnp.where(colk < WINDOW, pen, 0.0)
        sink = sinkc_ref[kv]
        mx = jnp.maximum(jnp.max(s, axis=1, keepdims=True), sink)
        p = jnp.exp(s - mx).astype(BF16)
        esink = jnp.exp(sink - mx)
        hr = (GROUP // 2) * r
        o_even = _bdot(p[:hr], jnp.concatenate([v_lo, ones], axis=1))
        o_odd = _bdot(p[hr:], jnp.concatenate([v_hi, ones], axis=1))
        for m in range(GROUP // 2):
            c = kv * (GROUP // 2) + m
            rs = slice(m * r, (m + 1) * r)
            ne = o_even[rs, :V7X_LANES] / (o_even[rs, V7X_LANES:] + esink[m * r:(m + 1) * r])
            no = o_odd[rs, :V7X_LANES] / (o_odd[rs, V7X_LANES:] + esink[hr + m * r:hr + (m + 1) * r])
            o_view[:, c * V7X_LANES:(c + 1) * V7X_LANES] = ne + no


def _attn_prompt_kernel(sink_ref, x_ref, kvc_ref, kvp_ref, g_ref, wq_ref, wo_ref, y_ref,
                        kvcat_ref, o_ref, bias_ref, sinkc_ref, wq_b, wo_b, *, att_layer):
    bi = pl.program_id(0)
    ti = pl.program_id(1)
    tq = x_ref.shape[0]

    @pl.when((bi == 0) & (ti == 0))
    def _():
        _init_attn_tables(bias_ref, sinkc_ref, sink_ref, att_layer, WINDOW)
        wq_b[...] = wq_ref[...].astype(BF16)
        wo_b[...] = wo_ref[...].astype(BF16)

    x = x_ref[...]
    hn = _rms(x, g_ref[...]).astype(BF16)
    q = _bdot(hn, wq_b[...]) * (HEAD_DIM ** -0.5)
    kvcat_ref[0:WINDOW, :] = kvp_ref[...].astype(BF16)
    kvcat_ref[WINDOW:, :] = kvc_ref[...].astype(BF16)
    pen = jnp.where(ti == 0, NEG, 0.0).astype(F32)
    for rb in range(tq // WINDOW):
        rows = slice(rb * WINDOW, rb * WINDOW + NK)
        _attn_unit(q[rb * WINDOW:(rb + 1) * WINDOW],
                   kvcat_ref[rows, 0:V7X_LANES], kvcat_ref[rows, V7X_LANES:2 * V7X_LANES],
                   bias_ref, sinkc_ref, o_ref.at[rb * WINDOW:(rb + 1) * WINDOW, :],
                   pen if rb == 0 else None)
    y_ref[...] = x + _bdot(o_ref[...].astype(BF16), wo_b[...])


def _attn_prompt(x3, kv3, p, att_layer, *, tq):
    b, t, _ = x3.shape
    sub = tq // WINDOW
    est = (4 * tq * D * 4 + 2 * D * D * (4 + 2) + 2 * (tq + WINDOW) * 2 * V7X_LANES * 4
           + N_KV * GROUP * WINDOW * (NK + V7X_LANES) * 4 + 10 * tq * D * 4)
    layer = N_REC + att_layer
    return pl.pallas_call(
        functools.partial(_attn_prompt_kernel, att_layer=att_layer),
        grid=(b, t // tq),
        in_specs=[
            pl.BlockSpec(memory_space=pltpu.SMEM),
            pl.BlockSpec((None, tq, D), lambda i, j: (i, j, 0)),
            pl.BlockSpec((None, tq, 2 * V7X_LANES), lambda i, j: (i, j, 0)),
            pl.BlockSpec((None, WINDOW, 2 * V7X_LANES),
                         lambda i, j: (i, jnp.maximum(j * sub - 1, 0), 0)),
            pl.BlockSpec((None, 1, D), lambda i, j: (layer, 0, 0)),
            _resident((None, D, D), lambda i, j: (att_layer, 0, 0)),
            _resident((None, D, D), lambda i, j: (att_layer, 0, 0)),
        ],
        out_specs=pl.BlockSpec((None, tq, D), lambda i, j: (i, j, 0)),
        out_shape=jax.ShapeDtypeStruct((b, t, D), F32),
        scratch_shapes=[
            pltpu.VMEM((tq + WINDOW, 2 * V7X_LANES), BF16),
            pltpu.VMEM((tq, D), F32),
            pltpu.VMEM((N_KV, GROUP * WINDOW, NK), F32),
            pltpu.VMEM((N_KV, GROUP * WINDOW, 1), F32),
            pltpu.VMEM((D, D), BF16),
            pltpu.VMEM((D, D), BF16),
        ],
        compiler_params=_params(est, 2),
        name="attn_prompt",
    )(p["attn_sinks"], x3, kv3, kv3, p["mix_norm"], p["attn_w_q"], p["attn_w_o"])


def _attn_sample_kernel(sink_ref, x_ref, k_ref, v_ref, g_ref, wq_ref, wo_ref, y_ref,
                        q_ref, o_ref, bias_ref, sinkc_ref, wq_b, wo_b, *, att_layer):
    bb, tq, _ = x_ref.shape

    @pl.when(pl.program_id(0) == 0)
    def _():
        _init_attn_tables(bias_ref, sinkc_ref, sink_ref, att_layer, tq)
        wq_b[...] = wq_ref[...].astype(BF16)
        wo_b[...] = wo_ref[...].astype(BF16)

    x = x_ref[...].reshape(bb * tq, D)
    hn = _rms(x, g_ref[...]).astype(BF16)
    q_ref[...] = (_bdot(hn, wq_b[...]) * (HEAD_DIM ** -0.5)).reshape(bb, tq, D)

    def one_sequence(i, carry):
        _attn_unit(q_ref[i], k_ref[i], v_ref[i], bias_ref, sinkc_ref, o_ref.at[i], None)
        return carry

    lax.fori_loop(0, bb, one_sequence, 0)
    y = x + _bdot(o_ref[...].reshape(bb * tq, D).astype(BF16), wo_b[...])
    y_ref[...] = y.reshape(bb, tq, D)


def _attn_sample(x3, k_all, v_all, p, att_layer, *, bb):
    b, tq, _ = x3.shape
    est = 2 * D * D * (4 + 2) + 12 * bb * tq * D * 4 + 4 * bb * NK * V7X_LANES * 2
    layer = N_REC + att_layer
    return pl.pallas_call(
        functools.partial(_attn_sample_kernel, att_layer=att_layer),
        grid=(b // bb,),
        in_specs=[
            pl.BlockSpec(memory_space=pltpu.SMEM),
            pl.BlockSpec((bb, tq, D), lambda i: (i, 0, 0)),
            pl.BlockSpec((bb, NK, V7X_LANES), lambda i: (i, 0, 0)),
            pl.BlockSpec((bb, NK, V7X_LANES), lambda i: (i, 0, 0)),
            pl.BlockSpec((None, 1, D), lambda i: (layer, 0, 0)),
            _resident((None, D, D), lambda i: (att_layer, 0, 0)),
            _resident((None, D, D), lambda i: (att_layer, 0, 0)),
        ],
        out_specs=pl.BlockSpec((bb, tq, D), lambda i: (i, 0, 0)),
        out_shape=jax.ShapeDtypeStruct((b, tq, D), F32),
        scratch_shapes=[
            pltpu.VMEM((bb, tq, D), F32),
            pltpu.VMEM((bb, tq, D), F32),
            pltpu.VMEM((N_KV, GROUP * tq, NK), F32),
            pltpu.VMEM((N_KV, GROUP * tq, 1), F32),
            pltpu.VMEM((D, D), BF16),
            pltpu.VMEM((D, D), BF16),
        ],
        compiler_params=_params(est, 1),
        name="attn_sample",
    )(p["attn_sinks"], x3, k_all, v_all, p["mix_norm"], p["attn_w_q"], p["attn_w_o"])


def _prepare_params(ffn1_norm, ffn1_w_gu, ffn1_w_down, mix_norm, ffn2_norm, ffn2_w_gu, ffn2_w_down,
                    rg_w_in, rg_conv_w, rg_conv_b, rg_w_a, rg_b_a, rg_w_x, rg_b_x, rg_lambda,
                    rg_w_out, kv_norm, w_kv, attn_w_q, attn_sinks, attn_w_o, final_norm):
    def vec(v):
        return v.reshape(v.shape[0], 1, D)

    return dict(
        ffn1_norm=vec(ffn1_norm), ffn1_w_gu=_cast_gu(ffn1_w_gu), ffn1_w_down=_cast_down(ffn1_w_down),
        ffn2_norm=vec(ffn2_norm), ffn2_w_gu=_cast_gu(ffn2_w_gu), ffn2_w_down=_cast_down(ffn2_w_down),
        mix_norm=vec(mix_norm),
        rg_w_in=rg_w_in, rg_conv_w=rg_conv_w, rg_conv_b=vec(rg_conv_b),
        rg_w_a=rg_w_a, rg_b_a=vec(rg_b_a), rg_w_x=rg_w_x,
        rg_b_x=vec(rg_b_x), rg_lambda=vec(rg_lambda), rg_w_out=rg_w_out,
        kv_norm=kv_norm.reshape(1, D), w_kv=w_kv,
        attn_w_q=attn_w_q, attn_sinks=attn_sinks, attn_w_o=attn_w_o,
        final_norm=final_norm.reshape(1, D),
    )


def _trunk(x, rg_h0, rg_buf0, cache_k, cache_v, p, *, ffn_tm, rg_bb, rg_tt, att_tile):
    b, t, _ = x.shape
    is_prompt = cache_k is None
    rg_h, rg_buf = [], []
    k_att = v_att = kv3 = new_k = new_v = None
    for l in range(N_LAYERS):
        x = _ffn(x.reshape(b * t, D), p["ffn1_norm"], p["ffn1_w_gu"], p["ffn1_w_down"],
                 p["final_norm"], l, final_norm=False, tm=ffn_tm).reshape(b, t, D)
        if l < N_REC:
            x, hl, bl = _rg_block(x, rg_h0[l].reshape(b, 1, D), rg_buf0[l], p, l, bb=rg_bb, tt=rg_tt)
            rg_h.append(hl.reshape(b, D))
            rg_buf.append(bl)
        elif is_prompt:
            x = _attn_prompt(x, kv3, p, l - N_REC, tq=att_tile)
        else:
            x = _attn_sample(x, k_att, v_att, p, l - N_REC, bb=att_tile)
        x = _ffn(x.reshape(b * t, D), p["ffn2_norm"], p["ffn2_w_gu"], p["ffn2_w_down"],
                 p["final_norm"], l, final_norm=(l == N_LAYERS - 1), tm=ffn_tm).reshape(b, t, D)
        if l == N_REC - 1:
            kv = _kv_proj(x.reshape(b * t, D), p["kv_norm"], p["w_kv"], tm=ffn_tm)
            kv3 = kv.reshape(b, t, 2 * N_KV * HEAD_DIM)
            k_new, v_new = kv3[..., :N_KV * HEAD_DIM], kv3[..., N_KV * HEAD_DIM:]
            if is_prompt:
                wb = min(WINDOW, t)
                new_k, new_v = k_new[:, t - wb:], v_new[:, t - wb:]
            else:
                wb = cache_k.shape[1]
                ck = cache_k.reshape(b, wb, N_KV * HEAD_DIM)
                cv = cache_v.reshape(b, wb, N_KV * HEAD_DIM)
                k_cat = jnp.concatenate([ck, k_new], axis=1)
                v_cat = jnp.concatenate([cv, v_new], axis=1)
                new_k, new_v = k_cat[:, t:], v_cat[:, t:]
                pad = ((0, 0), (0, NK - wb - t), (0, 0))
                k_att = jnp.pad(k_cat.astype(BF16), pad)
                v_att = jnp.pad(v_cat.astype(BF16), pad)
            new_k = new_k.reshape(b, wb, N_KV, HEAD_DIM)
            new_v = new_v.reshape(b, wb, N_KV, HEAD_DIM)
    return x, jnp.stack(rg_h), jnp.stack(rg_buf), new_k, new_v


def kernel(x_prompt, x_sample, state_rg_h, state_rg_conv, cache_k, cache_v, ffn1_norm, ffn1_w_gu, ffn1_w_down, mix_norm, ffn2_norm, ffn2_w_gu, ffn2_w_down, rg_w_in, rg_conv_w, rg_conv_b, rg_w_a, rg_b_a, rg_w_x, rg_b_x, rg_lambda, rg_w_out, kv_norm, w_kv, attn_w_q, attn_sinks, attn_w_o, final_norm):
    p = _prepare_params(ffn1_norm, ffn1_w_gu, ffn1_w_down, mix_norm, ffn2_norm, ffn2_w_gu,
                        ffn2_w_down, rg_w_in, rg_conv_w, rg_conv_b, rg_w_a, rg_b_a, rg_w_x, rg_b_x,
                        rg_lambda, rg_w_out, kv_norm, w_kv, attn_w_q, attn_sinks, attn_w_o,
                        final_norm)
    b = x_prompt.shape[0]
    h0 = jnp.zeros((N_REC, b, D), x_prompt.dtype)
    buf0 = jnp.zeros((N_REC, b, CONV_W - 1, D), x_prompt.dtype)
    y_p, p_h, p_conv, p_k, p_v = _trunk(x_prompt, h0, buf0, None, None, p,
                                        ffn_tm=1024, rg_bb=1, rg_tt=512, att_tile=512)
    y_s, s_h, s_conv, s_k, s_v = _trunk(x_sample, state_rg_h, state_rg_conv, cache_k, cache_v, p,
                                        ffn_tm=1024, rg_bb=32, rg_tt=8, att_tile=32)
    return (y_p, y_s, p_h, p_conv, p_k, p_v, s_h, s_conv, s_k, s_v)
```

```python
import functools
import math

import jax
import jax.numpy as jnp
from jax import lax
from jax.experimental import pallas as pl
from jax.experimental.pallas import tpu as pltpu

D = 1024
D_FF = 2816
N_LAYERS = 4
N_REC = 2
LRU_BLOCKS = 4
LRU_BW = D // LRU_BLOCKS
CONV_W = 4
LRU_C = 8.0
HEAD_DIM = 64
N_HEADS = D // HEAD_DIM
N_KV = 2
GROUP = N_HEADS // N_KV
WINDOW = 128
EPS = 1e-6
NEG = -1e30
Q_SCALE = HEAD_DIM ** -0.5

V7X_LANES = 128
V7X_SUBLANES = 8
V7X_VMEM_BYTES = 64 * 1024 * 1024

FF_CHUNK = 256
N_FF_CHUNKS = D_FF // FF_CHUNK
NK = 2 * WINDOW
SAMPLE_SEQS_PER_ITER = 4

BF16 = jnp.bfloat16
F32 = jnp.float32


def _vmem_limit(estimate_bytes):
    return int(min(max(estimate_bytes, 16 * 1024 * 1024), V7X_VMEM_BYTES - 6 * 1024 * 1024))


def _params(estimate_bytes, n_axes):
    return pltpu.CompilerParams(dimension_semantics=("arbitrary",) * n_axes,
                                vmem_limit_bytes=_vmem_limit(estimate_bytes))


def _resident(block_shape, index_map):
    return pl.BlockSpec(block_shape, index_map, pipeline_mode=pl.Buffered(1))


def _rms(x, g):
    ms = jnp.mean(x * x, axis=-1, keepdims=True)
    return x * lax.rsqrt(ms + EPS) * g


def _bdot(a, b):
    return jnp.dot(a, b, preferred_element_type=F32)


def _cast_gu_kernel(g_ref, u_ref, o_ref):
    o_ref[:, :FF_CHUNK] = g_ref[...].astype(BF16)
    o_ref[:, FF_CHUNK:] = u_ref[...].astype(BF16)


def _cast_gu(w):
    return pl.pallas_call(
        _cast_gu_kernel,
        grid=(N_LAYERS, N_FF_CHUNKS),
        in_specs=[pl.BlockSpec((None, D, FF_CHUNK), lambda l, j: (l, 0, j)),
                  pl.BlockSpec((None, D, FF_CHUNK), lambda l, j: (l, 0, N_FF_CHUNKS + j))],
        out_specs=pl.BlockSpec((None, None, D, 2 * FF_CHUNK), lambda l, j: (l, j, 0, 0)),
        out_shape=jax.ShapeDtypeStruct((N_LAYERS, N_FF_CHUNKS, D, 2 * FF_CHUNK), BF16),
        compiler_params=_params(8 * D * FF_CHUNK * 4, 2),
        name="cast_gu",
    )(w, w)


def _cast_kernel(w_ref, o_ref):
    o_ref[...] = w_ref[...].astype(BF16)


def _cast_down(w):
    rows = D_FF // 2
    out = pl.pallas_call(
        _cast_kernel,
        grid=(N_LAYERS, D_FF // rows),
        in_specs=[pl.BlockSpec((None, rows, D), lambda l, j: (l, j, 0))],
        out_specs=pl.BlockSpec((None, rows, D), lambda l, j: (l, j, 0)),
        out_shape=jax.ShapeDtypeStruct((N_LAYERS, D_FF, D), BF16),
        compiler_params=_params(6 * rows * D * 4, 2),
        name="cast_down",
    )(w)
    return out.reshape(N_LAYERS, N_FF_CHUNKS, FF_CHUNK, D)


def _ffn_kernel(x_ref, g_ref, wgu_ref, wd_ref, fg_ref, o_ref, n_ref, h_ref, *, final_norm):
    n_ref[...] = _rms(x_ref[...], g_ref[...]).astype(BF16)
    for j in range(N_FF_CHUNKS):
        gu = _bdot(n_ref[...], wgu_ref[j])
        gate, up = gu[:, :FF_CHUNK], gu[:, FF_CHUNK:]
        h_ref[j] = (gate * jax.nn.sigmoid(gate) * up).astype(BF16)
    acc = _bdot(h_ref[0], wd_ref[0])
    for j in range(1, N_FF_CHUNKS):
        acc = acc + _bdot(h_ref[j], wd_ref[j])
    y = x_ref[...] + 0.5 * acc
    if final_norm:
        y = _rms(y, fg_ref[...])
    o_ref[...] = y


def _ffn(x2, norm, wgu, wd, fgain, layer, *, final_norm, tm):
    t = x2.shape[0]
    est = (4 * tm * D * 4 + tm * D * 2 + tm * D_FF * 2 + 3 * D * D_FF * 2
           + 3 * tm * 2 * FF_CHUNK * 4 + 2 * tm * D * 4)
    return pl.pallas_call(
        functools.partial(_ffn_kernel, final_norm=final_norm),
        grid=(t // tm,),
        in_specs=[
            pl.BlockSpec((tm, D), lambda i: (i, 0)),
            pl.BlockSpec((None, 1, D), lambda i: (layer, 0, 0)),
            _resident((None, N_FF_CHUNKS, D, 2 * FF_CHUNK), lambda i: (layer, 0, 0, 0)),
            _resident((None, N_FF_CHUNKS, FF_CHUNK, D), lambda i: (layer, 0, 0, 0)),
            pl.BlockSpec((1, D), lambda i: (0, 0)),
        ],
        out_specs=pl.BlockSpec((tm, D), lambda i: (i, 0)),
        out_shape=jax.ShapeDtypeStruct((t, D), F32),
        scratch_shapes=[pltpu.VMEM((tm, D), BF16), pltpu.VMEM((N_FF_CHUNKS, tm, FF_CHUNK), BF16)],
        compiler_params=_params(est, 1),
        name="ffn",
    )(x2, norm, wgu, wd, fgain)


def _softplus(z):
    return jnp.maximum(z, 0.0) + jnp.log1p(jnp.exp(-jnp.abs(z)))


def _gelu_tanh(z):
    return 0.5 * z * (1.0 + jnp.tanh(math.sqrt(2.0 / math.pi) * (z + 0.044715 * (z * z * z))))


def _rg_kernel(x_ref, h0_ref, buf0_ref, g_ref, win_ref, cw_ref, cb_ref, wa_ref, ba_ref, wx_ref,
               bx_ref, lam_ref, wout_ref, y_ref, hl_ref, bufo_ref,
               ext_ref, hc_ref, a_ref, u_ref, hs_ref, gpre_ref,
               win_b, wa_b, wx_b, wout_b):
    t = pl.program_id(1)
    bb, tt, _ = x_ref.shape
    rows = bb * tt
    hist = V7X_SUBLANES

    @pl.when((pl.program_id(0) == 0) & (t == 0))
    def _():
        win_b[...] = win_ref[...].astype(BF16)
        wa_b[...] = wa_ref[...].astype(BF16)
        wx_b[...] = wx_ref[...].astype(BF16)
        wout_b[...] = wout_ref[...].astype(BF16)

    @pl.when(t == 0)
    def _():
        ext_ref[:, hist - (CONV_W - 1):hist, :] = buf0_ref[...]
        hc_ref[...] = h0_ref[...]

    x = x_ref[...].reshape(rows, D)
    hn = _rms(x, g_ref[...]).astype(BF16)
    proj = _bdot(hn, win_b[...])
    gate, xr = proj[:, :D], proj[:, D:]
    ext_ref[:, hist:hist + tt, :] = xr.reshape(bb, tt, D)

    xc = cb_ref[...].reshape(1, 1, D)
    for k in range(CONV_W):
        lo = hist - (CONV_W - 1) + k
        xc = xc + ext_ref[:, lo:lo + tt, :] * cw_ref[k:k + 1, :].reshape(1, 1, D)
    xc = xc.reshape(rows, D)

    xcb = xc.astype(BF16)
    for n in range(LRU_BLOCKS):
        sl = slice(n * LRU_BW, (n + 1) * LRU_BW)
        gpre_ref[0, :, sl] = _bdot(xcb[:, sl], wa_b[n])
        gpre_ref[1, :, sl] = _bdot(xcb[:, sl], wx_b[n])
    r = jax.nn.sigmoid(gpre_ref[0] + ba_ref[...])
    i = jax.nn.sigmoid(gpre_ref[1] + bx_ref[...])
    log_a = (-LRU_C) * r * _softplus(-lam_ref[...])
    a = jnp.exp(log_a)
    u = jnp.sqrt(-jnp.tanh(log_a) * (a * a + 1.0)) * (i * xc)

    a3 = a.reshape(rows // V7X_SUBLANES, V7X_SUBLANES, D)
    u3 = u.reshape(rows // V7X_SUBLANES, V7X_SUBLANES, D)
    tin = lax.broadcasted_iota(jnp.int32, a3.shape, 1)
    for s in (1, 2, 4):
        a_prev = pltpu.roll(a3, s, 1)
        u_prev = pltpu.roll(u3, s, 1)
        keep = tin >= s
        u3 = jnp.where(keep, a3 * u_prev + u3, u3)
        a3 = jnp.where(keep, a3 * a_prev, a3)
    a_ref[...] = a3.reshape(bb, tt, D)
    u_ref[...] = u3.reshape(bb, tt, D)

    carry = hc_ref[...]
    for gi in range(tt // V7X_SUBLANES):
        sl = slice(gi * V7X_SUBLANES, (gi + 1) * V7X_SUBLANES)
        h = a_ref[:, sl, :] * carry + u_ref[:, sl, :]
        hs_ref[:, sl, :] = h
        carry = h[:, V7X_SUBLANES - 1:, :]
    hc_ref[...] = carry

    hseq = hs_ref[...].reshape(rows, D)
    y = _bdot((hseq * _gelu_tanh(gate)).astype(BF16), wout_b[...])
    y_ref[...] = (x + y).reshape(bb, tt, D)

    hl_ref[...] = carry
    bufo_ref[...] = ext_ref[:, hist + tt - (CONV_W - 1):hist + tt, :]
    ext_ref[:, 0:hist, :] = ext_ref[:, tt:tt + hist, :]


def _rg_block(x3, h0, buf0, p, layer, *, bb, tt):
    b, t, _ = x3.shape
    rows = bb * tt
    n_w = 2 * D * D + D * D + 2 * LRU_BLOCKS * LRU_BW * LRU_BW
    est = 4 * rows * D * 4 + n_w * (4 + 2) + 16 * rows * D * 4
    vec = pl.BlockSpec((None, 1, D), lambda i, j: (layer, 0, 0))
    gates = _resident((None, LRU_BLOCKS, LRU_BW, LRU_BW), lambda i, j: (layer, 0, 0, 0))
    return pl.pallas_call(
        _rg_kernel,
        grid=(b // bb, t // tt),
        in_specs=[
            pl.BlockSpec((bb, tt, D), lambda i, j: (i, j, 0)),
            pl.BlockSpec((bb, 1, D), lambda i, j: (i, 0, 0)),
            pl.BlockSpec((bb, CONV_W - 1, D), lambda i, j: (i, 0, 0)),
            vec,
            _resident((None, D, 2 * D), lambda i, j: (layer, 0, 0)),
            pl.BlockSpec((None, CONV_W, D), lambda i, j: (layer, 0, 0)),
            vec,
            gates, vec, gates, vec, vec,
            _resident((None, D, D), lambda i, j: (layer, 0, 0)),
        ],
        out_specs=[
            pl.BlockSpec((bb, tt, D), lambda i, j: (i, j, 0)),
            pl.BlockSpec((bb, 1, D), lambda i, j: (i, 0, 0)),
            pl.BlockSpec((bb, CONV_W - 1, D), lambda i, j: (i, 0, 0)),
        ],
        out_shape=[
            jax.ShapeDtypeStruct((b, t, D), F32),
            jax.ShapeDtypeStruct((b, 1, D), F32),
            jax.ShapeDtypeStruct((b, CONV_W - 1, D), F32),
        ],
        scratch_shapes=[
            pltpu.VMEM((bb, tt + V7X_SUBLANES, D), F32),
            pltpu.VMEM((bb, 1, D), F32),
            pltpu.VMEM((bb, tt, D), F32),
            pltpu.VMEM((bb, tt, D), F32),
            pltpu.VMEM((bb, tt, D), F32),
            pltpu.VMEM((2, rows, D), F32),
            pltpu.VMEM((D, 2 * D), BF16),
            pltpu.VMEM((LRU_BLOCKS, LRU_BW, LRU_BW), BF16),
            pltpu.VMEM((LRU_BLOCKS, LRU_BW, LRU_BW), BF16),
            pltpu.VMEM((D, D), BF16),
        ],
        compiler_params=_params(est, 2),
        name="rg_block",
    )(x3, h0, buf0, p["mix_norm"], p["rg_w_in"], p["rg_conv_w"], p["rg_conv_b"], p["rg_w_a"],
      p["rg_b_a"], p["rg_w_x"], p["rg_b_x"], p["rg_lambda"], p["rg_w_out"])


def _kv_kernel(x_ref, g_ref, w_ref, o_ref, w_b):
    @pl.when(pl.program_id(0) == 0)
    def _():
        w_b[...] = w_ref[...].astype(BF16)

    o_ref[...] = _bdot(_rms(x_ref[...], g_ref[...]).astype(BF16), w_b[...])


def _kv_proj(x2, norm, w, *, tm):
    t = x2.shape[0]
    n = w.shape[1]
    return pl.pallas_call(
        _kv_kernel,
        grid=(t // tm,),
        in_specs=[pl.BlockSpec((tm, D), lambda i: (i, 0)),
                  pl.BlockSpec((1, D), lambda i: (0, 0)),
                  _resident((D, n), lambda i: (0, 0))],
        out_specs=pl.BlockSpec((tm, n), lambda i: (i, 0)),
        out_shape=jax.ShapeDtypeStruct((t, n), F32),
        scratch_shapes=[pltpu.VMEM((D, n), BF16)],
        compiler_params=_params(6 * tm * D * 4, 1),
        name="kv_proj",
    )(x2, norm, w)


def _init_attn_tables(bias_ref, sinkc_ref, sink_ref, att_layer, r):
    shape = bias_ref.shape[1:]
    sh = int(math.log2(r))
    rho = lax.broadcasted_iota(jnp.int32, shape, 0)
    lane = lax.broadcasted_iota(jnp.int32, shape, 1)
    e = rho >> (sh + 2)
    m = (rho >> sh) & 3
    qi = rho & (r - 1)
    g = 2 * m + e
    diff = jnp.where(lane > qi, qi - lane + WINDOW, qi - lane)
    for kv in range(N_KV):
        head = (kv * GROUP + g + 1).astype(F32)
        slope = jnp.exp2(-0.5 * head)
        bias_ref[kv] = -slope * diff.astype(F32)
        gcol = g[:, :1]
        sc = jnp.zeros((shape[0], 1), F32)
        for gg in range(GROUP):
            sc = jnp.where(gcol == gg, sink_ref[att_layer, kv * GROUP + gg], sc)
        sinkc_ref[kv] = sc


def _attn_blocks(blocks, bias_ref, sinkc_ref):
    half = HEAD_DIM
    low = lax.broadcasted_iota(jnp.int32, (1, V7X_LANES), 1) < half
    zero_b = jnp.zeros((), BF16)
    units = [(blk, kv) for blk in range(len(blocks)) for kv in range(N_KV)]
    swapped = {}

    def lane_swapped(blk):
        if blk not in swapped:
            _, k2, v2, _, _ = blocks[blk]
            swapped[blk] = (pltpu.roll(k2, half, 1), pltpu.roll(v2, half, 1))
        return swapped[blk]

    def scores(blk, kv):
        q, k2, _, _, _ = blocks[blk]
        k_sw, _ = lane_swapped(blk)
        kdup = jnp.where(low, k2, k_sw) if kv == 0 else jnp.where(low, k_sw, k2)
        parts = []
        for e in range(2):
            for m in range(GROUP // 2):
                c = kv * (GROUP // 2) + m
                qc = q[:, c * V7X_LANES:(c + 1) * V7X_LANES]
                parts.append(jnp.where(low, qc, 0.0) if e == 0 else jnp.where(low, 0.0, qc))
        qst = jnp.concatenate(parts, axis=0).astype(BF16)
        return lax.dot_general(qst, kdup, (((1,), (1,)), ((), ())), preferred_element_type=F32)

    def finish(blk, kv, s):
        q, _, v2, o_view, pen = blocks[blk]
        r = q.shape[0]
        _, v_sw = lane_swapped(blk)
        v_lo = jnp.where(low, v2 if kv == 0 else v_sw, zero_b)
        v_hi = jnp.where(low, zero_b, v_sw if kv == 0 else v2)
        ones = jnp.ones(v2.shape, BF16)
        rho = lax.broadcasted_iota(jnp.int32, (GROUP * r, V7X_LANES), 0)
        lane = lax.broadcasted_iota(jnp.int32, (GROUP * r, V7X_LANES), 1)
        prev = lane > (rho & (r - 1))
        s = jnp.where(prev, s[:, :WINDOW], s[:, WINDOW:]) + bias_ref[kv]
        if pen is not None:
            s = s + jnp.where(prev, pen, 0.0)
        sink = sinkc_ref[kv]
        mx = jnp.maximum(jnp.max(s, axis=1, keepdims=True), sink)
        p = jnp.exp(s - mx)
        esink = jnp.exp(sink - mx)
        p = jnp.concatenate([jnp.where(prev, p, 0.0), jnp.where(prev, 0.0, p)], axis=1).astype(BF16)
        hr = (GROUP // 2) * r
        o_even = _bdot(p[:hr], jnp.concatenate([v_lo, ones], axis=1))
        o_odd = _bdot(p[hr:], jnp.concatenate([v_hi, ones], axis=1))
        for m in range(GROUP // 2):
            c = kv * (GROUP // 2) + m
            rs = slice(m * r, (m + 1) * r)
            ne = o_even[rs, :V7X_LANES] / (o_even[rs, V7X_LANES:] + esink[m * r:(m + 1) * r])
            no = o_odd[rs, :V7X_LANES] / (o_odd[rs, V7X_LANES:] + esink[hr + m * r:hr + (m + 1) * r])
            o_view[:, c * V7X_LANES:(c + 1) * V7X_LANES] = ne + no

    for unit in units:
        finish(*unit, scores(*unit))


def _attn_prompt_kernel(sink_ref, x_ref, kvc_ref, kvp_ref, g_ref, wq_ref, wo_ref, y_ref,
                        kvcat_ref, o_ref, bias_ref, sinkc_ref, wq_b, wo_b, *, att_layer):
    bi = pl.program_id(0)
    ti = pl.program_id(1)
    tq = x_ref.shape[0]

    @pl.when((bi == 0) & (ti == 0))
    def _():
        _init_attn_tables(bias_ref, sinkc_ref, sink_ref, att_layer, WINDOW)
        wq_b[...] = wq_ref[...].astype(BF16)
        wo_b[...] = wo_ref[...].astype(BF16)

    x = x_ref[...]
    hn = _rms(x, g_ref[...]).astype(BF16)
    q = _bdot(hn, wq_b[...]) * Q_SCALE
    kvcat_ref[0:WINDOW, :] = kvp_ref[...].astype(BF16)
    kvcat_ref[WINDOW:, :] = kvc_ref[...].astype(BF16)
    pen = jnp.where(ti == 0, NEG, 0.0).astype(F32)
    blocks = []
    for rb in range(tq // WINDOW):
        rows = slice(rb * WINDOW, rb * WINDOW + NK)
        blocks.append((q[rb * WINDOW:(rb + 1) * WINDOW],
                       kvcat_ref[rows, 0:V7X_LANES], kvcat_ref[rows, V7X_LANES:2 * V7X_LANES],
                       o_ref.at[rb * WINDOW:(rb + 1) * WINDOW, :], pen if rb == 0 else None))
    _attn_blocks(blocks, bias_ref, sinkc_ref)
    y_ref[...] = x + _bdot(o_ref[...].astype(BF16), wo_b[...])


def _attn_prompt(x3, kv3, p, att_layer, *, tq):
    b, t, _ = x3.shape
    sub = tq // WINDOW
    est = (4 * tq * D * 4 + 2 * D * D * (4 + 2) + 2 * (tq + WINDOW) * 2 * V7X_LANES * 4
           + N_KV * GROUP * WINDOW * 2 * V7X_LANES * 4 + 10 * tq * D * 4)
    layer = N_REC + att_layer
    return pl.pallas_call(
        functools.partial(_attn_prompt_kernel, att_layer=att_layer),
        grid=(b, t // tq),
        in_specs=[
            pl.BlockSpec(memory_space=pltpu.SMEM),
            pl.BlockSpec((None, tq, D), lambda i, j: (i, j, 0)),
            pl.BlockSpec((None, tq, 2 * V7X_LANES), lambda i, j: (i, j, 0)),
            pl.BlockSpec((None, WINDOW, 2 * V7X_LANES),
                         lambda i, j: (i, jnp.maximum(j * sub - 1, 0), 0)),
            pl.BlockSpec((None, 1, D), lambda i, j: (layer, 0, 0)),
            _resident((None, D, D), lambda i, j: (att_layer, 0, 0)),
            _resident((None, D, D), lambda i, j: (att_layer, 0, 0)),
        ],
        out_specs=pl.BlockSpec((None, tq, D), lambda i, j: (i, j, 0)),
        out_shape=jax.ShapeDtypeStruct((b, t, D), F32),
        scratch_shapes=[
            pltpu.VMEM((tq + WINDOW, 2 * V7X_LANES), BF16),
            pltpu.VMEM((tq, D), F32),
            pltpu.VMEM((N_KV, GROUP * WINDOW, V7X_LANES), F32),
            pltpu.VMEM((N_KV, GROUP * WINDOW, 1), F32),
            pltpu.VMEM((D, D), BF16),
            pltpu.VMEM((D, D), BF16),
        ],
        compiler_params=_params(est, 2),
        name="attn_prompt",
    )(p["attn_sinks"], x3, kv3, kv3, p["mix_norm"], p["attn_w_q"], p["attn_w_o"])


def _attn_sample_kernel(sink_ref, x_ref, k_ref, v_ref, g_ref, wq_ref, wo_ref, y_ref,
                        q_ref, o_ref, bias_ref, sinkc_ref, wq_b, wo_b, *, att_layer):
    bb, tq, _ = x_ref.shape

    @pl.when(pl.program_id(0) == 0)
    def _():
        _init_attn_tables(bias_ref, sinkc_ref, sink_ref, att_layer, tq)
        wq_b[...] = wq_ref[...].astype(BF16)
        wo_b[...] = wo_ref[...].astype(BF16)

    x = x_ref[...].reshape(bb * tq, D)
    hn = _rms(x, g_ref[...]).astype(BF16)
    q_ref[...] = (_bdot(hn, wq_b[...]) * Q_SCALE).reshape(bb, tq, D)

    def some_sequences(g, carry):
        blocks = []
        for d in range(SAMPLE_SEQS_PER_ITER):
            i = g * SAMPLE_SEQS_PER_ITER + d
            blocks.append((q_ref[i], k_ref[i], v_ref[i], o_ref.at[i], None))
        _attn_blocks(blocks, bias_ref, sinkc_ref)
        return carry

    lax.fori_loop(0, bb // SAMPLE_SEQS_PER_ITER, some_sequences, 0)
    y = x + _bdot(o_ref[...].reshape(bb * tq, D).astype(BF16), wo_b[...])
    y_ref[...] = y.reshape(bb, tq, D)


def _attn_sample(x3, k_all, v_all, p, att_layer, *, bb):
    b, tq, _ = x3.shape
    est = 2 * D * D * (4 + 2) + 12 * bb * tq * D * 4 + 4 * bb * NK * V7X_LANES * 2
    layer = N_REC + att_layer
    return pl.pallas_call(
        functools.partial(_attn_sample_kernel, att_layer=att_layer),
        grid=(b // bb,),
        in_specs=[
            pl.BlockSpec(memory_space=pltpu.SMEM),
            pl.BlockSpec((bb, tq, D), lambda i: (i, 0, 0)),
            pl.BlockSpec((bb, NK, V7X_LANES), lambda i: (i, 0, 0)),
            pl.BlockSpec((bb, NK, V7X_LANES), lambda i: (i, 0, 0)),
            pl.BlockSpec((None, 1, D), lambda i: (layer, 0, 0)),
            _resident((None, D, D), lambda i: (att_layer, 0, 0)),
            _resident((None, D, D), lambda i: (att_layer, 0, 0)),
        ],
        out_specs=pl.BlockSpec((bb, tq, D), lambda i: (i, 0, 0)),
        out_shape=jax.ShapeDtypeStruct((b, tq, D), F32),
        scratch_shapes=[
            pltpu.VMEM((bb, tq, D), F32),
            pltpu.VMEM((bb, tq, D), F32),
            pltpu.VMEM((N_KV, GROUP * tq, V7X_LANES), F32),
            pltpu.VMEM((N_KV, GROUP * tq, 1), F32),
            pltpu.VMEM((D, D), BF16),
            pltpu.VMEM((D, D), BF16),
        ],
        compiler_params=_params(est, 1),
        name="attn_sample",
    )(p["attn_sinks"], x3, k_all, v_all, p["mix_norm"], p["attn_w_q"], p["attn_w_o"])


def _prepare_params(ffn1_norm, ffn1_w_gu, ffn1_w_down, mix_norm, ffn2_norm, ffn2_w_gu, ffn2_w_down,
                    rg_w_in, rg_conv_w, rg_conv_b, rg_w_a, rg_b_a, rg_w_x, rg_b_x, rg_lambda,
                    rg_w_out, kv_norm, w_kv, attn_w_q, attn_sinks, attn_w_o, final_norm):
    def vec(v):
        return v.reshape(v.shape[0], 1, D)

    return dict(
        ffn1_norm=vec(ffn1_norm), ffn1_w_gu=_cast_gu(ffn1_w_gu), ffn1_w_down=_cast_down(ffn1_w_down),
        ffn2_norm=vec(ffn2_norm), ffn2_w_gu=_cast_gu(ffn2_w_gu), ffn2_w_down=_cast_down(ffn2_w_down),
        mix_norm=vec(mix_norm),
        rg_w_in=rg_w_in, rg_conv_w=rg_conv_w, rg_conv_b=vec(rg_conv_b),
        rg_w_a=rg_w_a, rg_b_a=vec(rg_b_a), rg_w_x=rg_w_x,
        rg_b_x=vec(rg_b_x), rg_lambda=vec(rg_lambda), rg_w_out=rg_w_out,
        kv_norm=kv_norm.reshape(1, D), w_kv=w_kv,
        attn_w_q=attn_w_q, attn_sinks=attn_sinks, attn_w_o=attn_w_o,
        final_norm=final_norm.reshape(1, D),
    )


def _trunk(x, rg_h0, rg_buf0, cache_k, cache_v, p, *, ffn_tm, rg_bb, rg_tt, att_tile):
    b, t, _ = x.shape
    is_prompt = cache_k is None
    rg_h, rg_buf = [], []
    k_att = v_att = kv3 = new_k = new_v = None
    for l in range(N_LAYERS):
        x = _ffn(x.reshape(b * t, D), p["ffn1_norm"], p["ffn1_w_gu"], p["ffn1_w_down"],
                 p["final_norm"], l, final_norm=False, tm=ffn_tm).reshape(b, t, D)
        if l < N_REC:
            x, hl, bl = _rg_block(x, rg_h0[l].reshape(b, 1, D), rg_buf0[l], p, l, bb=rg_bb, tt=rg_tt)
            rg_h.append(hl.reshape(b, D))
            rg_buf.append(bl)
        elif is_prompt:
            x = _attn_prompt(x, kv3, p, l - N_REC, tq=att_tile)
        else:
            x = _attn_sample(x, k_att, v_att, p, l - N_REC, bb=att_tile)
        x = _ffn(x.reshape(b * t, D), p["ffn2_norm"], p["ffn2_w_gu"], p["ffn2_w_down"],
                 p["final_norm"], l, final_norm=(l == N_LAYERS - 1), tm=ffn_tm).reshape(b, t, D)
        if l == N_REC - 1:
            kv = _kv_proj(x.reshape(b * t, D), p["kv_norm"], p["w_kv"], tm=ffn_tm)
            kv3 = kv.reshape(b, t, 2 * N_KV * HEAD_DIM)
            k_new, v_new = kv3[..., :N_KV * HEAD_DIM], kv3[..., N_KV * HEAD_DIM:]
            if is_prompt:
                wb = min(WINDOW, t)
                new_k, new_v = k_new[:, t - wb:], v_new[:, t - wb:]
            else:
                wb = cache_k.shape[1]
                ck = cache_k.reshape(b, wb, N_KV * HEAD_DIM)
                cv = cache_v.reshape(b, wb, N_KV * HEAD_DIM)
                k_cat = jnp.concatenate([ck, k_new], axis=1)
                v_cat = jnp.concatenate([cv, v_new], axis=1)
                new_k, new_v = k_cat[:, t:], v_cat[:, t:]
                pad = ((0, 0), (0, NK - wb - t), (0, 0))
                k_att = jnp.pad(k_cat.astype(BF16), pad)
                v_att = jnp.pad(v_cat.astype(BF16), pad)
            new_k = new_k.reshape(b, wb, N_KV, HEAD_DIM)
            new_v = new_v.reshape(b, wb, N_KV, HEAD_DIM)
    return x, jnp.stack(rg_h), jnp.stack(rg_buf), new_k, new_v


def kernel(x_prompt, x_sample, state_rg_h, state_rg_conv, cache_k, cache_v, ffn1_norm, ffn1_w_gu, ffn1_w_down, mix_norm, ffn2_norm, ffn2_w_gu, ffn2_w_down, rg_w_in, rg_conv_w, rg_conv_b, rg_w_a, rg_b_a, rg_w_x, rg_b_x, rg_lambda, rg_w_out, kv_norm, w_kv, attn_w_q, attn_sinks, attn_w_o, final_norm):
    p = _prepare_params(ffn1_norm, ffn1_w_gu, ffn1_w_down, mix_norm, ffn2_norm, ffn2_w_gu,
                        ffn2_w_down, rg_w_in, rg_conv_w, rg_conv_b, rg_w_a, rg_b_a, rg_w_x, rg_b_x,
                        rg_lambda, rg_w_out, kv_norm, w_kv, attn_w_q, attn_sinks, attn_w_o,
                        final_norm)
    b = x_prompt.shape[0]
    h0 = jnp.zeros((N_REC, b, D), x_prompt.dtype)
    buf0 = jnp.zeros((N_REC, b, CONV_W - 1, D), x_prompt.dtype)
    y_p, p_h, p_conv, p_k, p_v = _trunk(x_prompt, h0, buf0, None, None, p,
                                        ffn_tm=1024, rg_bb=1, rg_tt=512, att_tile=512)
    y_s, s_h, s_conv, s_k, s_v = _trunk(x_sample, state_rg_h, state_rg_conv, cache_k, cache_v, p,
                                        ffn_tm=1024, rg_bb=32, rg_tt=8, att_tile=32)
    return (y_p, y_s, p_h, p_conv, p_k, p_v, s_h, s_conv, s_k, s_v)
```

```python
import functools
import math

import jax
import jax.numpy as jnp
from jax import lax
from jax.experimental import pallas as pl
from jax.experimental.pallas import tpu as pltpu

D = 1024
D_FF = 2816
N_LAYERS = 4
N_REC = 2
LRU_BLOCKS = 4
LRU_BW = D // LRU_BLOCKS
CONV_W = 4
LRU_C = 8.0
HEAD_DIM = 64
N_HEADS = D // HEAD_DIM
N_KV = 2
GROUP = N_HEADS // N_KV
WINDOW = 128
EPS = 1e-6
NEG = -1e30
Q_SCALE = HEAD_DIM ** -0.5

V7X_LANES = 128
V7X_SUBLANES = 8
V7X_VMEM_BYTES = 64 * 1024 * 1024

FF_CHUNK = 256
N_FF_CHUNKS = D_FF // FF_CHUNK
NK = 2 * WINDOW
SAMPLE_SEQS_PER_ITER = 8

BF16 = jnp.bfloat16
F32 = jnp.float32


def _vmem_limit(estimate_bytes):
    return int(min(max(estimate_bytes, 16 * 1024 * 1024), V7X_VMEM_BYTES - 6 * 1024 * 1024))


def _params(estimate_bytes, n_axes):
    return pltpu.CompilerParams(dimension_semantics=("arbitrary",) * n_axes,
                                vmem_limit_bytes=_vmem_limit(estimate_bytes))


def _resident(block_shape, index_map):
    return pl.BlockSpec(block_shape, index_map, pipeline_mode=pl.Buffered(1))


def _rms(x, g):
    ms = jnp.mean(x * x, axis=-1, keepdims=True)
    return x * lax.rsqrt(ms + EPS) * g


def _bdot(a, b):
    return jnp.dot(a, b, preferred_element_type=F32)


def _cast_gu_kernel(g_ref, u_ref, o_ref):
    o_ref[:, :FF_CHUNK] = g_ref[...].astype(BF16)
    o_ref[:, FF_CHUNK:] = u_ref[...].astype(BF16)


def _cast_gu(w):
    return pl.pallas_call(
        _cast_gu_kernel,
        grid=(N_LAYERS, N_FF_CHUNKS),
        in_specs=[pl.BlockSpec((None, D, FF_CHUNK), lambda l, j: (l, 0, j)),
                  pl.BlockSpec((None, D, FF_CHUNK), lambda l, j: (l, 0, N_FF_CHUNKS + j))],
        out_specs=pl.BlockSpec((None, None, D, 2 * FF_CHUNK), lambda l, j: (l, j, 0, 0)),
        out_shape=jax.ShapeDtypeStruct((N_LAYERS, N_FF_CHUNKS, D, 2 * FF_CHUNK), BF16),
        compiler_params=_params(8 * D * FF_CHUNK * 4, 2),
        name="cast_gu",
    )(w, w)


def _cast_kernel(w_ref, o_ref):
    o_ref[...] = w_ref[...].astype(BF16)


def _cast_down(w):
    rows = D_FF // 2
    out = pl.pallas_call(
        _cast_kernel,
        grid=(N_LAYERS, D_FF // rows),
        in_specs=[pl.BlockSpec((None, rows, D), lambda l, j: (l, j, 0))],
        out_specs=pl.BlockSpec((None, rows, D), lambda l, j: (l, j, 0)),
        out_shape=jax.ShapeDtypeStruct((N_LAYERS, D_FF, D), BF16),
        compiler_params=_params(6 * rows * D * 4, 2),
        name="cast_down",
    )(w)
    return out.reshape(N_LAYERS, N_FF_CHUNKS, FF_CHUNK, D)


def _ffn_kernel(x_ref, g_ref, wgu_ref, wd_ref, fg_ref, o_ref, n_ref, h_ref, *, final_norm):
    n_ref[...] = _rms(x_ref[...], g_ref[...]).astype(BF16)
    for j in range(N_FF_CHUNKS):
        gu = _bdot(n_ref[...], wgu_ref[j])
        gate, up = gu[:, :FF_CHUNK], gu[:, FF_CHUNK:]
        h_ref[j] = (gate * jax.nn.sigmoid(gate) * up).astype(BF16)
    acc = _bdot(h_ref[0], wd_ref[0])
    for j in range(1, N_FF_CHUNKS):
        acc = acc + _bdot(h_ref[j], wd_ref[j])
    y = x_ref[...] + 0.5 * acc
    if final_norm:
        y = _rms(y, fg_ref[...])
    o_ref[...] = y


def _ffn(x2, norm, wgu, wd, fgain, layer, *, final_norm, tm):
    t = x2.shape[0]
    est = (4 * tm * D * 4 + tm * D * 2 + tm * D_FF * 2 + 3 * D * D_FF * 2
           + 3 * tm * 2 * FF_CHUNK * 4 + 2 * tm * D * 4)
    return pl.pallas_call(
        functools.partial(_ffn_kernel, final_norm=final_norm),
        grid=(t // tm,),
        in_specs=[
            pl.BlockSpec((tm, D), lambda i: (i, 0)),
            pl.BlockSpec((None, 1, D), lambda i: (layer, 0, 0)),
            _resident((None, N_FF_CHUNKS, D, 2 * FF_CHUNK), lambda i: (layer, 0, 0, 0)),
            _resident((None, N_FF_CHUNKS, FF_CHUNK, D), lambda i: (layer, 0, 0, 0)),
            pl.BlockSpec((1, D), lambda i: (0, 0)),
        ],
        out_specs=pl.BlockSpec((tm, D), lambda i: (i, 0)),
        out_shape=jax.ShapeDtypeStruct((t, D), F32),
        scratch_shapes=[pltpu.VMEM((tm, D), BF16), pltpu.VMEM((N_FF_CHUNKS, tm, FF_CHUNK), BF16)],
        compiler_params=_params(est, 1),
        name="ffn",
    )(x2, norm, wgu, wd, fgain)


def _softplus(z):
    return jnp.maximum(z, 0.0) + jnp.log1p(jnp.exp(-jnp.abs(z)))


def _gelu_tanh(z):
    return 0.5 * z * (1.0 + jnp.tanh(math.sqrt(2.0 / math.pi) * (z + 0.044715 * (z * z * z))))


def _rg_kernel(x_ref, h0_ref, buf0_ref, g_ref, win_ref, cw_ref, cb_ref, wa_ref, ba_ref, wx_ref,
               bx_ref, lam_ref, wout_ref, y_ref, hl_ref, bufo_ref,
               ext_ref, hc_ref, a_ref, u_ref, hs_ref, gpre_ref,
               win_b, wa_b, wx_b, wout_b):
    t = pl.program_id(1)
    bb, tt, _ = x_ref.shape
    rows = bb * tt
    hist = V7X_SUBLANES

    @pl.when((pl.program_id(0) == 0) & (t == 0))
    def _():
        win_b[...] = win_ref[...].astype(BF16)
        wa_b[...] = wa_ref[...].astype(BF16)
        wx_b[...] = wx_ref[...].astype(BF16)
        wout_b[...] = wout_ref[...].astype(BF16)

    @pl.when(t == 0)
    def _():
        ext_ref[:, hist - (CONV_W - 1):hist, :] = buf0_ref[...]
        hc_ref[...] = h0_ref[...]

    x = x_ref[...].reshape(rows, D)
    hn = _rms(x, g_ref[...]).astype(BF16)
    proj = _bdot(hn, win_b[...])
    gate, xr = proj[:, :D], proj[:, D:]
    ext_ref[:, hist:hist + tt, :] = xr.reshape(bb, tt, D)

    xc = cb_ref[...].reshape(1, 1, D)
    for k in range(CONV_W):
        lo = hist - (CONV_W - 1) + k
        xc = xc + ext_ref[:, lo:lo + tt, :] * cw_ref[k:k + 1, :].reshape(1, 1, D)
    xc = xc.reshape(rows, D)

    xcb = xc.astype(BF16)
    for n in range(LRU_BLOCKS):
        sl = slice(n * LRU_BW, (n + 1) * LRU_BW)
        gpre_ref[0, :, sl] = _bdot(xcb[:, sl], wa_b[n])
        gpre_ref[1, :, sl] = _bdot(xcb[:, sl], wx_b[n])
    r = jax.nn.sigmoid(gpre_ref[0] + ba_ref[...])
    i = jax.nn.sigmoid(gpre_ref[1] + bx_ref[...])
    log_a = (-LRU_C) * r * _softplus(-lam_ref[...])
    a = jnp.exp(log_a)
    u = jnp.sqrt(-jnp.tanh(log_a) * (a * a + 1.0)) * (i * xc)

    a3 = a.reshape(rows // V7X_SUBLANES, V7X_SUBLANES, D)
    u3 = u.reshape(rows // V7X_SUBLANES, V7X_SUBLANES, D)
    tin = lax.broadcasted_iota(jnp.int32, a3.shape, 1)
    for s in (1, 2, 4):
        a_prev = pltpu.roll(a3, s, 1)
        u_prev = pltpu.roll(u3, s, 1)
        keep = tin >= s
        u3 = jnp.where(keep, a3 * u_prev + u3, u3)
        a3 = jnp.where(keep, a3 * a_prev, a3)
    a_ref[...] = a3.reshape(bb, tt, D)
    u_ref[...] = u3.reshape(bb, tt, D)

    carry = hc_ref[...]
    for gi in range(tt // V7X_SUBLANES):
        sl = slice(gi * V7X_SUBLANES, (gi + 1) * V7X_SUBLANES)
        h = a_ref[:, sl, :] * carry + u_ref[:, sl, :]
        hs_ref[:, sl, :] = h
        carry = h[:, V7X_SUBLANES - 1:, :]
    hc_ref[...] = carry

    hseq = hs_ref[...].reshape(rows, D)
    y = _bdot((hseq * _gelu_tanh(gate)).astype(BF16), wout_b[...])
    y_ref[...] = (x + y).reshape(bb, tt, D)

    hl_ref[...] = carry
    bufo_ref[...] = ext_ref[:, hist + tt - (CONV_W - 1):hist + tt, :]
    ext_ref[:, 0:hist, :] = ext_ref[:, tt:tt + hist, :]


def _rg_block(x3, h0, buf0, p, layer, *, bb, tt):
    b, t, _ = x3.shape
    rows = bb * tt
    n_w = 2 * D * D + D * D + 2 * LRU_BLOCKS * LRU_BW * LRU_BW
    est = 4 * rows * D * 4 + n_w * (4 + 2) + 16 * rows * D * 4
    vec = pl.BlockSpec((None, 1, D), lambda i, j: (layer, 0, 0))
    gates = _resident((None, LRU_BLOCKS, LRU_BW, LRU_BW), lambda i, j: (layer, 0, 0, 0))
    return pl.pallas_call(
        _rg_kernel,
        grid=(b // bb, t // tt),
        in_specs=[
            pl.BlockSpec((bb, tt, D), lambda i, j: (i, j, 0)),
            pl.BlockSpec((bb, 1, D), lambda i, j: (i, 0, 0)),
            pl.BlockSpec((bb, CONV_W - 1, D), lambda i, j: (i, 0, 0)),
            vec,
            _resident((None, D, 2 * D), lambda i, j: (layer, 0, 0)),
            pl.BlockSpec((None, CONV_W, D), lambda i, j: (layer, 0, 0)),
            vec,
            gates, vec, gates, vec, vec,
            _resident((None, D, D), lambda i, j: (layer, 0, 0)),
        ],
        out_specs=[
            pl.BlockSpec((bb, tt, D), lambda i, j: (i, j, 0)),
            pl.BlockSpec((bb, 1, D), lambda i, j: (i, 0, 0)),
            pl.BlockSpec((bb, CONV_W - 1, D), lambda i, j: (i, 0, 0)),
        ],
        out_shape=[
            jax.ShapeDtypeStruct((b, t, D), F32),
            jax.ShapeDtypeStruct((b, 1, D), F32),
            jax.ShapeDtypeStruct((b, CONV_W - 1, D), F32),
        ],
        scratch_shapes=[
            pltpu.VMEM((bb, tt + V7X_SUBLANES, D), F32),
            pltpu.VMEM((bb, 1, D), F32),
            pltpu.VMEM((bb, tt, D), F32),
            pltpu.VMEM((bb, tt, D), F32),
            pltpu.VMEM((bb, tt, D), F32),
            pltpu.VMEM((2, rows, D), F32),
            pltpu.VMEM((D, 2 * D), BF16),
            pltpu.VMEM((LRU_BLOCKS, LRU_BW, LRU_BW), BF16),
            pltpu.VMEM((LRU_BLOCKS, LRU_BW, LRU_BW), BF16),
            pltpu.VMEM((D, D), BF16),
        ],
        compiler_params=_params(est, 2),
        name="rg_block",
    )(x3, h0, buf0, p["mix_norm"], p["rg_w_in"], p["rg_conv_w"], p["rg_conv_b"], p["rg_w_a"],
      p["rg_b_a"], p["rg_w_x"], p["rg_b_x"], p["rg_lambda"], p["rg_w_out"])


def _kv_kernel(x_ref, g_ref, w_ref, o_ref, w_b):
    @pl.when(pl.program_id(0) == 0)
    def _():
        w_b[...] = w_ref[...].astype(BF16)

    o_ref[...] = _bdot(_rms(x_ref[...], g_ref[...]).astype(BF16), w_b[...])


def _kv_proj(x2, norm, w, *, tm):
    t = x2.shape[0]
    n = w.shape[1]
    return pl.pallas_call(
        _kv_kernel,
        grid=(t // tm,),
        in_specs=[pl.BlockSpec((tm, D), lambda i: (i, 0)),
                  pl.BlockSpec((1, D), lambda i: (0, 0)),
                  _resident((D, n), lambda i: (0, 0))],
        out_specs=pl.BlockSpec((tm, n), lambda i: (i, 0)),
        out_shape=jax.ShapeDtypeStruct((t, n), F32),
        scratch_shapes=[pltpu.VMEM((D, n), BF16)],
        compiler_params=_params(6 * tm * D * 4, 1),
        name="kv_proj",
    )(x2, norm, w)


def _init_attn_tables(bias_ref, sinkc_ref, sink_ref, att_layer, r):
    shape = bias_ref.shape[1:]
    sh = int(math.log2(r))
    rho = lax.broadcasted_iota(jnp.int32, shape, 0)
    lane = lax.broadcasted_iota(jnp.int32, shape, 1)
    e = rho >> (sh + 2)
    m = (rho >> sh) & 3
    qi = rho & (r - 1)
    g = 2 * m + e
    diff = jnp.where(lane > qi, qi - lane + WINDOW, qi - lane)
    for kv in range(N_KV):
        head = (kv * GROUP + g + 1).astype(F32)
        slope = jnp.exp2(-0.5 * head)
        bias_ref[kv] = -slope * diff.astype(F32)
        gcol = g[:, :1]
        sc = jnp.zeros((shape[0], 1), F32)
        for gg in range(GROUP):
            sc = jnp.where(gcol == gg, sink_ref[att_layer, kv * GROUP + gg], sc)
        sinkc_ref[kv] = sc


def _attn_blocks(blocks, bias_ref, sinkc_ref, *, one_pass):
    half = HEAD_DIM
    low = lax.broadcasted_iota(jnp.int32, (1, V7X_LANES), 1) < half
    zero_b = jnp.zeros((), BF16)
    units = [(blk, kv) for blk in range(len(blocks)) for kv in range(N_KV)]
    swapped = {}

    def lane_swapped(blk):
        if blk not in swapped:
            _, k2, v2, _, _ = blocks[blk]
            swapped[blk] = (pltpu.roll(k2, half, 1), pltpu.roll(v2, half, 1))
        return swapped[blk]

    def scores(blk, kv):
        q, k2, _, _, _ = blocks[blk]
        k_sw, _ = lane_swapped(blk)
        kdup = jnp.where(low, k2, k_sw) if kv == 0 else jnp.where(low, k_sw, k2)
        parts = []
        for e in range(2):
            for m in range(GROUP // 2):
                c = kv * (GROUP // 2) + m
                qc = q[:, c * V7X_LANES:(c + 1) * V7X_LANES]
                parts.append(jnp.where(low, qc, 0.0) if e == 0 else jnp.where(low, 0.0, qc))
        qst = jnp.concatenate(parts, axis=0).astype(BF16)
        return lax.dot_general(qst, kdup, (((1,), (1,)), ((), ())), preferred_element_type=F32)

    def weighted_values(blk, kv, p, esink):
        q, _, v2, o_view, _ = blocks[blk]
        r = q.shape[0]
        _, v_sw = lane_swapped(blk)
        v_lo = jnp.where(low, v2 if kv == 0 else v_sw, zero_b)
        v_hi = jnp.where(low, zero_b, v_sw if kv == 0 else v2)
        ones = jnp.ones(v2.shape, BF16)
        hr = (GROUP // 2) * r
        o_even = _bdot(p[:hr], jnp.concatenate([v_lo, ones], axis=1))
        o_odd = _bdot(p[hr:], jnp.concatenate([v_hi, ones], axis=1))
        for m in range(GROUP // 2):
            c = kv * (GROUP // 2) + m
            rs = slice(m * r, (m + 1) * r)
            ne = o_even[rs, :V7X_LANES] / (o_even[rs, V7X_LANES:] + esink[m * r:(m + 1) * r])
            no = o_odd[rs, :V7X_LANES] / (o_odd[rs, V7X_LANES:] + esink[hr + m * r:hr + (m + 1) * r])
            o_view[:, c * V7X_LANES:(c + 1) * V7X_LANES] = ne + no

    r = blocks[0][0].shape[0]
    ur = GROUP * r

    def softmax_pass(group):
        n_rows = len(group) * ur
        rho = lax.broadcasted_iota(jnp.int32, (n_rows, V7X_LANES), 0)
        lane = lax.broadcasted_iota(jnp.int32, (n_rows, V7X_LANES), 1)
        prev = lane > (rho & (r - 1))
        s = jnp.concatenate([scores(*unit) for unit in group], axis=0)
        bias = jnp.concatenate([bias_ref[kv] for _, kv in group], axis=0)
        s = jnp.where(prev, s[:, :WINDOW], s[:, WINDOW:]) + bias
        if any(blocks[b][4] is not None for b, _ in group):
            zero_pen = jnp.zeros((), F32)
            pens = [jnp.full((ur, 1), zero_pen if blocks[b][4] is None else blocks[b][4], F32)
                    for b, _ in group]
            s = s + jnp.where(prev, jnp.concatenate(pens, axis=0), 0.0)
        sink = jnp.concatenate([sinkc_ref[kv] for _, kv in group], axis=0)
        mx = jnp.maximum(jnp.max(s, axis=1, keepdims=True), sink)
        p = jnp.exp(s - mx)
        esink = jnp.exp(sink - mx)
        p = jnp.concatenate([jnp.where(prev, p, 0.0), jnp.where(prev, 0.0, p)], axis=1).astype(BF16)
        for u, unit in enumerate(group):
            weighted_values(*unit, p[u * ur:(u + 1) * ur], esink[u * ur:(u + 1) * ur])

    if one_pass:
        softmax_pass(units)
    else:
        for unit in units:
            softmax_pass([unit])


def _attn_prompt_kernel(sink_ref, x_ref, kvc_ref, kvp_ref, g_ref, wq_ref, wo_ref, y_ref,
                        kvcat_ref, o_ref, bias_ref, sinkc_ref, wq_b, wo_b, *, att_layer):
    bi = pl.program_id(0)
    ti = pl.program_id(1)
    tq = x_ref.shape[0]

    @pl.when((bi == 0) & (ti == 0))
    def _():
        _init_attn_tables(bias_ref, sinkc_ref, sink_ref, att_layer, WINDOW)
        wq_b[...] = wq_ref[...].astype(BF16)
        wo_b[...] = wo_ref[...].astype(BF16)

    x = x_ref[...]
    hn = _rms(x, g_ref[...]).astype(BF16)
    q = _bdot(hn, wq_b[...]) * Q_SCALE
    kvcat_ref[0:WINDOW, :] = kvp_ref[...].astype(BF16)
    kvcat_ref[WINDOW:, :] = kvc_ref[...].astype(BF16)
    pen = jnp.where(ti == 0, NEG, 0.0).astype(F32)
    blocks = []
    for rb in range(tq // WINDOW):
        rows = slice(rb * WINDOW, rb * WINDOW + NK)
        blocks.append((q[rb * WINDOW:(rb + 1) * WINDOW],
                       kvcat_ref[rows, 0:V7X_LANES], kvcat_ref[rows, V7X_LANES:2 * V7X_LANES],
                       o_ref.at[rb * WINDOW:(rb + 1) * WINDOW, :], pen if rb == 0 else None))
    _attn_blocks(blocks, bias_ref, sinkc_ref, one_pass=False)
    y_ref[...] = x + _bdot(o_ref[...].astype(BF16), wo_b[...])


def _attn_prompt(x3, kv3, p, att_layer, *, tq):
    b, t, _ = x3.shape
    sub = tq // WINDOW
    est = (4 * tq * D * 4 + 2 * D * D * (4 + 2) + 2 * (tq + WINDOW) * 2 * V7X_LANES * 4
           + N_KV * GROUP * WINDOW * 2 * V7X_LANES * 4 + 10 * tq * D * 4)
    layer = N_REC + att_layer
    return pl.pallas_call(
        functools.partial(_attn_prompt_kernel, att_layer=att_layer),
        grid=(b, t // tq),
        in_specs=[
            pl.BlockSpec(memory_space=pltpu.SMEM),
            pl.BlockSpec((None, tq, D), lambda i, j: (i, j, 0)),
            pl.BlockSpec((None, tq, 2 * V7X_LANES), lambda i, j: (i, j, 0)),
            pl.BlockSpec((None, WINDOW, 2 * V7X_LANES),
                         lambda i, j: (i, jnp.maximum(j * sub - 1, 0), 0)),
            pl.BlockSpec((None, 1, D), lambda i, j: (layer, 0, 0)),
            _resident((None, D, D), lambda i, j: (att_layer, 0, 0)),
            _resident((None, D, D), lambda i, j: (att_layer, 0, 0)),
        ],
        out_specs=pl.BlockSpec((None, tq, D), lambda i, j: (i, j, 0)),
        out_shape=jax.ShapeDtypeStruct((b, t, D), F32),
        scratch_shapes=[
            pltpu.VMEM((tq + WINDOW, 2 * V7X_LANES), BF16),
            pltpu.VMEM((tq, D), F32),
            pltpu.VMEM((N_KV, GROUP * WINDOW, V7X_LANES), F32),
            pltpu.VMEM((N_KV, GROUP * WINDOW, 1), F32),
            pltpu.VMEM((D, D), BF16),
            pltpu.VMEM((D, D), BF16),
        ],
        compiler_params=_params(est, 2),
        name="attn_prompt",
    )(p["attn_sinks"], x3, kv3, kv3, p["mix_norm"], p["attn_w_q"], p["attn_w_o"])


def _attn_sample_kernel(sink_ref, x_ref, k_ref, v_ref, g_ref, wq_ref, wo_ref, y_ref,
                        q_ref, o_ref, bias_ref, sinkc_ref, wq_b, wo_b, *, att_layer):
    bb, tq, _ = x_ref.shape

    @pl.when(pl.program_id(0) == 0)
    def _():
        _init_attn_tables(bias_ref, sinkc_ref, sink_ref, att_layer, tq)
        wq_b[...] = wq_ref[...].astype(BF16)
        wo_b[...] = wo_ref[...].astype(BF16)

    x = x_ref[...].reshape(bb * tq, D)
    hn = _rms(x, g_ref[...]).astype(BF16)
    q_ref[...] = (_bdot(hn, wq_b[...]) * Q_SCALE).reshape(bb, tq, D)

    def some_sequences(g, carry):
        blocks = []
        for d in range(SAMPLE_SEQS_PER_ITER):
            i = g * SAMPLE_SEQS_PER_ITER + d
            blocks.append((q_ref[i], k_ref[i], v_ref[i], o_ref.at[i], None))
        _attn_blocks(blocks, bias_ref, sinkc_ref, one_pass=True)
        return carry

    lax.fori_loop(0, bb // SAMPLE_SEQS_PER_ITER, some_sequences, 0)
    y = x + _bdot(o_ref[...].reshape(bb * tq, D).astype(BF16), wo_b[...])
    y_ref[...] = y.reshape(bb, tq, D)


def _attn_sample(x3, k_all, v_all, p, att_layer, *, bb):
    b, tq, _ = x3.shape
    est = 2 * D * D * (4 + 2) + 12 * bb * tq * D * 4 + 4 * bb * NK * V7X_LANES * 2
    layer = N_REC + att_layer
    return pl.pallas_call(
        functools.partial(_attn_sample_kernel, att_layer=att_layer),
        grid=(b // bb,),
        in_specs=[
            pl.BlockSpec(memory_space=pltpu.SMEM),
            pl.BlockSpec((bb, tq, D), lambda i: (i, 0, 0)),
            pl.BlockSpec((bb, NK, V7X_LANES), lambda i: (i, 0, 0)),
            pl.BlockSpec((bb, NK, V7X_LANES), lambda i: (i, 0, 0)),
            pl.BlockSpec((None, 1, D), lambda i: (layer, 0, 0)),
            _resident((None, D, D), lambda i: (att_layer, 0, 0)),
            _resident((None, D, D), lambda i: (att_layer, 0, 0)),
        ],
        out_specs=pl.BlockSpec((bb, tq, D), lambda i: (i, 0, 0)),
        out_shape=jax.ShapeDtypeStruct((b, tq, D), F32),
        scratch_shapes=[
            pltpu.VMEM((bb, tq, D), F32),
            pltpu.VMEM((bb, tq, D), F32),
            pltpu.VMEM((N_KV, GROUP * tq, V7X_LANES), F32),
            pltpu.VMEM((N_KV, GROUP * tq, 1), F32),
            pltpu.VMEM((D, D), BF16),
            pltpu.VMEM((D, D), BF16),
        ],
        compiler_params=_params(est, 1),
        name="attn_sample",
    )(p["attn_sinks"], x3, k_all, v_all, p["mix_norm"], p["attn_w_q"], p["attn_w_o"])


def _prepare_params(ffn1_norm, ffn1_w_gu, ffn1_w_down, mix_norm, ffn2_norm, ffn2_w_gu, ffn2_w_down,
                    rg_w_in, rg_conv_w, rg_conv_b, rg_w_a, rg_b_a, rg_w_x, rg_b_x, rg_lambda,
                    rg_w_out, kv_norm, w_kv, attn_w_q, attn_sinks, attn_w_o, final_norm):
    def vec(v):
        return v.reshape(v.shape[0], 1, D)

    return dict(
        ffn1_norm=vec(ffn1_norm), ffn1_w_gu=_cast_gu(ffn1_w_gu), ffn1_w_down=_cast_down(ffn1_w_down),
        ffn2_norm=vec(ffn2_norm), ffn2_w_gu=_cast_gu(ffn2_w_gu), ffn2_w_down=_cast_down(ffn2_w_down),
        mix_norm=vec(mix_norm),
        rg_w_in=rg_w_in, rg_conv_w=rg_conv_w, rg_conv_b=vec(rg_conv_b),
        rg_w_a=rg_w_a, rg_b_a=vec(rg_b_a), rg_w_x=rg_w_x,
        rg_b_x=vec(rg_b_x), rg_lambda=vec(rg_lambda), rg_w_out=rg_w_out,
        kv_norm=kv_norm.reshape(1, D), w_kv=w_kv,
        attn_w_q=attn_w_q, attn_sinks=attn_sinks, attn_w_o=attn_w_o,
        final_norm=final_norm.reshape(1, D),
    )


def _trunk(x, rg_h0, rg_buf0, cache_k, cache_v, p, *, ffn_tm, rg_bb, rg_tt, att_tile):
    b, t, _ = x.shape
    is_prompt = cache_k is None
    rg_h, rg_buf = [], []
    k_att = v_att = kv3 = new_k = new_v = None
    for l in range(N_LAYERS):
        x = _ffn(x.reshape(b * t, D), p["ffn1_norm"], p["ffn1_w_gu"], p["ffn1_w_down"],
                 p["final_norm"], l, final_norm=False, tm=ffn_tm).reshape(b, t, D)
        if l < N_REC:
            x, hl, bl = _rg_block(x, rg_h0[l].reshape(b, 1, D), rg_buf0[l], p, l, bb=rg_bb, tt=rg_tt)
            rg_h.append(hl.reshape(b, D))
            rg_buf.append(bl)
        elif is_prompt:
            x = _attn_prompt(x, kv3, p, l - N_REC, tq=att_tile)
        else:
            x = _attn_sample(x, k_att, v_att, p, l - N_REC, bb=att_tile)
        x = _ffn(x.reshape(b * t, D), p["ffn2_norm"], p["ffn2_w_gu"], p["ffn2_w_down"],
                 p["final_norm"], l, final_norm=(l == N_LAYERS - 1), tm=ffn_tm).reshape(b, t, D)
        if l == N_REC - 1:
            kv = _kv_proj(x.reshape(b * t, D), p["kv_norm"], p["w_kv"], tm=ffn_tm)
            kv3 = kv.reshape(b, t, 2 * N_KV * HEAD_DIM)
            k_new, v_new = kv3[..., :N_KV * HEAD_DIM], kv3[..., N_KV * HEAD_DIM:]
            if is_prompt:
                wb = min(WINDOW, t)
                new_k, new_v = k_new[:, t - wb:], v_new[:, t - wb:]
            else:
                wb = cache_k.shape[1]
                ck = cache_k.reshape(b, wb, N_KV * HEAD_DIM)
                cv = cache_v.reshape(b, wb, N_KV * HEAD_DIM)
                k_cat = jnp.concatenate([ck, k_new], axis=1)
                v_cat = jnp.concatenate([cv, v_new], axis=1)
                new_k, new_v = k_cat[:, t:], v_cat[:, t:]
                pad = ((0, 0), (0, NK - wb - t), (0, 0))
                k_att = jnp.pad(k_cat.astype(BF16), pad)
                v_att = jnp.pad(v_cat.astype(BF16), pad)
            new_k = new_k.reshape(b, wb, N_KV, HEAD_DIM)
            new_v = new_v.reshape(b, wb, N_KV, HEAD_DIM)
    return x, jnp.stack(rg_h), jnp.stack(rg_buf), new_k, new_v


def kernel(x_prompt, x_sample, state_rg_h, state_rg_conv, cache_k, cache_v, ffn1_norm, ffn1_w_gu, ffn1_w_down, mix_norm, ffn2_norm, ffn2_w_gu, ffn2_w_down, rg_w_in, rg_conv_w, rg_conv_b, rg_w_a, rg_b_a, rg_w_x, rg_b_x, rg_lambda, rg_w_out, kv_norm, w_kv, attn_w_q, attn_sinks, attn_w_o, final_norm):
    p = _prepare_params(ffn1_norm, ffn1_w_gu, ffn1_w_down, mix_norm, ffn2_norm, ffn2_w_gu,
                        ffn2_w_down, rg_w_in, rg_conv_w, rg_conv_b, rg_w_a, rg_b_a, rg_w_x, rg_b_x,
                        rg_lambda, rg_w_out, kv_norm, w_kv, attn_w_q, attn_sinks, attn_w_o,
                        final_norm)
    b = x_prompt.shape[0]
    h0 = jnp.zeros((N_REC, b, D), x_prompt.dtype)
    buf0 = jnp.zeros((N_REC, b, CONV_W - 1, D), x_prompt.dtype)
    y_p, p_h, p_conv, p_k, p_v = _trunk(x_prompt, h0, buf0, None, None, p,
                                        ffn_tm=1024, rg_bb=1, rg_tt=512, att_tile=512)
    y_s, s_h, s_conv, s_k, s_v = _trunk(x_sample, state_rg_h, state_rg_conv, cache_k, cache_v, p,
                                        ffn_tm=1024, rg_bb=32, rg_tt=8, att_tile=32)
    return (y_p, y_s, p_h, p_conv, p_k, p_v, s_h, s_conv, s_k, s_v)
```

```python
import functools
import math

import jax
import jax.numpy as jnp
from jax import lax
from jax.experimental import pallas as pl
from jax.experimental.pallas import tpu as pltpu

D = 1024
D_FF = 2816
N_LAYERS = 4
N_REC = 2
LRU_BLOCKS = 4
LRU_BW = D // LRU_BLOCKS
CONV_W = 4
LRU_C = 8.0
HEAD_DIM = 64
N_HEADS = D // HEAD_DIM
N_KV = 2
GROUP = N_HEADS // N_KV
WINDOW = 128
EPS = 1e-6
NEG = -1e30
Q_SCALE = HEAD_DIM ** -0.5

V7X_LANES = 128
V7X_SUBLANES = 8
V7X_VMEM_BYTES = 64 * 1024 * 1024

FF_CHUNK = 256
N_FF_CHUNKS = D_FF // FF_CHUNK
NK = 2 * WINDOW
SAMPLE_SEQS_PER_ITER = 16

BF16 = jnp.bfloat16
F32 = jnp.float32


def _vmem_limit(estimate_bytes):
    return int(min(max(estimate_bytes, 16 * 1024 * 1024), V7X_VMEM_BYTES - 6 * 1024 * 1024))


def _params(estimate_bytes, n_axes):
    return pltpu.CompilerParams(dimension_semantics=("arbitrary",) * n_axes,
                                vmem_limit_bytes=_vmem_limit(estimate_bytes))


def _resident(block_shape, index_map):
    return pl.BlockSpec(block_shape, index_map, pipeline_mode=pl.Buffered(1))


def _rms(x, g):
    ms = jnp.mean(x * x, axis=-1, keepdims=True)
    return x * lax.rsqrt(ms + EPS) * g


def _bdot(a, b):
    return jnp.dot(a, b, preferred_element_type=F32)


def _cast_gu_kernel(w_ref, o_ref):
    for j in range(N_FF_CHUNKS):
        gate = slice(j * FF_CHUNK, (j + 1) * FF_CHUNK)
        up = slice(D_FF + j * FF_CHUNK, D_FF + (j + 1) * FF_CHUNK)
        o_ref[j, :, :FF_CHUNK] = w_ref[:, gate].astype(BF16)
        o_ref[j, :, FF_CHUNK:] = w_ref[:, up].astype(BF16)


def _cast_gu(w):
    rows = D // 4
    return pl.pallas_call(
        _cast_gu_kernel,
        grid=(N_LAYERS, D // rows),
        in_specs=[pl.BlockSpec((None, rows, 2 * D_FF), lambda l, r: (l, r, 0))],
        out_specs=pl.BlockSpec((None, N_FF_CHUNKS, rows, 2 * FF_CHUNK), lambda l, r: (l, 0, r, 0)),
        out_shape=jax.ShapeDtypeStruct((N_LAYERS, N_FF_CHUNKS, D, 2 * FF_CHUNK), BF16),
        compiler_params=_params(2 * rows * 2 * D_FF * (4 + 2) + 2 * rows * 2 * D_FF * 4, 2),
        name="cast_gu",
    )(w)


def _cast_kernel(w_ref, o_ref):
    o_ref[...] = w_ref[...].astype(BF16)


def _cast_down(w):
    rows = D_FF // 2
    out = pl.pallas_call(
        _cast_kernel,
        grid=(N_LAYERS, D_FF // rows),
        in_specs=[pl.BlockSpec((None, rows, D), lambda l, j: (l, j, 0))],
        out_specs=pl.BlockSpec((None, rows, D), lambda l, j: (l, j, 0)),
        out_shape=jax.ShapeDtypeStruct((N_LAYERS, D_FF, D), BF16),
        compiler_params=_params(6 * rows * D * 4, 2),
        name="cast_down",
    )(w)
    return out.reshape(N_LAYERS, N_FF_CHUNKS, FF_CHUNK, D)


def _ffn_kernel(x_ref, g_ref, wgu_ref, wd_ref, fg_ref, o_ref, n_ref, h_ref, *, final_norm):
    n_ref[...] = _rms(x_ref[...], g_ref[...]).astype(BF16)
    for j in range(N_FF_CHUNKS):
        gu = _bdot(n_ref[...], wgu_ref[j])
        gate, up = gu[:, :FF_CHUNK], gu[:, FF_CHUNK:]
        h_ref[j] = (gate * jax.nn.sigmoid(gate) * up).astype(BF16)
    acc = _bdot(h_ref[0], wd_ref[0])
    for j in range(1, N_FF_CHUNKS):
        acc = acc + _bdot(h_ref[j], wd_ref[j])
    y = x_ref[...] + 0.5 * acc
    if final_norm:
        y = _rms(y, fg_ref[...])
    o_ref[...] = y


def _ffn(x2, norm, wgu, wd, fgain, layer, *, final_norm, tm):
    t = x2.shape[0]
    est = (4 * tm * D * 4 + tm * D * 2 + tm * D_FF * 2 + 3 * D * D_FF * 2
           + 3 * tm * 2 * FF_CHUNK * 4 + 2 * tm * D * 4)
    return pl.pallas_call(
        functools.partial(_ffn_kernel, final_norm=final_norm),
        grid=(t // tm,),
        in_specs=[
            pl.BlockSpec((tm, D), lambda i: (i, 0)),
            pl.BlockSpec((None, 1, D), lambda i: (layer, 0, 0)),
            _resident((None, N_FF_CHUNKS, D, 2 * FF_CHUNK), lambda i: (layer, 0, 0, 0)),
            _resident((None, N_FF_CHUNKS, FF_CHUNK, D), lambda i: (layer, 0, 0, 0)),
            pl.BlockSpec((1, D), lambda i: (0, 0)),
        ],
        out_specs=pl.BlockSpec((tm, D), lambda i: (i, 0)),
        out_shape=jax.ShapeDtypeStruct((t, D), F32),
        scratch_shapes=[pltpu.VMEM((tm, D), BF16), pltpu.VMEM((N_FF_CHUNKS, tm, FF_CHUNK), BF16)],
        compiler_params=_params(est, 1),
        name="ffn",
    )(x2, norm, wgu, wd, fgain)


def _softplus(z):
    return jnp.maximum(z, 0.0) + jnp.log1p(jnp.exp(-jnp.abs(z)))


def _gelu_tanh(z):
    return 0.5 * z * (1.0 + jnp.tanh(math.sqrt(2.0 / math.pi) * (z + 0.044715 * (z * z * z))))


def _rg_kernel(x_ref, h0_ref, buf0_ref, g_ref, win_ref, cw_ref, cb_ref, wa_ref, ba_ref, wx_ref,
               bx_ref, lam_ref, wout_ref, y_ref, hl_ref, bufo_ref,
               ext_ref, hc_ref, a_ref, u_ref, hs_ref, gpre_ref,
               win_b, wa_b, wx_b, wout_b):
    t = pl.program_id(1)
    bb, tt, _ = x_ref.shape
    rows = bb * tt
    hist = V7X_SUBLANES

    @pl.when((pl.program_id(0) == 0) & (t == 0))
    def _():
        win_b[...] = win_ref[...].astype(BF16)
        wa_b[...] = wa_ref[...].astype(BF16)
        wx_b[...] = wx_ref[...].astype(BF16)
        wout_b[...] = wout_ref[...].astype(BF16)

    @pl.when(t == 0)
    def _():
        ext_ref[:, hist - (CONV_W - 1):hist, :] = buf0_ref[...]
        hc_ref[...] = h0_ref[...]

    x = x_ref[...].reshape(rows, D)
    hn = _rms(x, g_ref[...]).astype(BF16)
    proj = _bdot(hn, win_b[...])
    gate, xr = proj[:, :D], proj[:, D:]
    ext_ref[:, hist:hist + tt, :] = xr.reshape(bb, tt, D)

    xc = cb_ref[...].reshape(1, 1, D)
    for k in range(CONV_W):
        lo = hist - (CONV_W - 1) + k
        xc = xc + ext_ref[:, lo:lo + tt, :] * cw_ref[k:k + 1, :].reshape(1, 1, D)
    xc = xc.reshape(rows, D)

    xcb = xc.astype(BF16)
    for n in range(LRU_BLOCKS):
        sl = slice(n * LRU_BW, (n + 1) * LRU_BW)
        gpre_ref[0, :, sl] = _bdot(xcb[:, sl], wa_b[n])
        gpre_ref[1, :, sl] = _bdot(xcb[:, sl], wx_b[n])
    r = jax.nn.sigmoid(gpre_ref[0] + ba_ref[...])
    i = jax.nn.sigmoid(gpre_ref[1] + bx_ref[...])
    log_a = (-LRU_C) * r * _softplus(-lam_ref[...])
    a = jnp.exp(log_a)
    u = jnp.sqrt(-jnp.tanh(log_a) * (a * a + 1.0)) * (i * xc)

    a3 = a.reshape(rows // V7X_SUBLANES, V7X_SUBLANES, D)
    u3 = u.reshape(rows // V7X_SUBLANES, V7X_SUBLANES, D)
    tin = lax.broadcasted_iota(jnp.int32, a3.shape, 1)
    for s in (1, 2, 4):
        a_prev = pltpu.roll(a3, s, 1)
        u_prev = pltpu.roll(u3, s, 1)
        keep = tin >= s
        u3 = jnp.where(keep, a3 * u_prev + u3, u3)
        a3 = jnp.where(keep, a3 * a_prev, a3)
    a_ref[...] = a3.reshape(bb, tt, D)
    u_ref[...] = u3.reshape(bb, tt, D)

    carry = hc_ref[...]
    for gi in range(tt // V7X_SUBLANES):
        sl = slice(gi * V7X_SUBLANES, (gi + 1) * V7X_SUBLANES)
        h = a_ref[:, sl, :] * carry + u_ref[:, sl, :]
        hs_ref[:, sl, :] = h
        carry = h[:, V7X_SUBLANES - 1:, :]
    hc_ref[...] = carry

    hseq = hs_ref[...].reshape(rows, D)
    y = _bdot((hseq * _gelu_tanh(gate)).astype(BF16), wout_b[...])
    y_ref[...] = (x + y).reshape(bb, tt, D)

    hl_ref[...] = carry
    bufo_ref[...] = ext_ref[:, hist + tt - (CONV_W - 1):hist + tt, :]
    ext_ref[:, 0:hist, :] = ext_ref[:, tt:tt + hist, :]


def _rg_block(x3, h0, buf0, p, layer, *, bb, tt):
    b, t, _ = x3.shape
    rows = bb * tt
    n_w = 2 * D * D + D * D + 2 * LRU_BLOCKS * LRU_BW * LRU_BW
    est = 4 * rows * D * 4 + n_w * (4 + 2) + 16 * rows * D * 4
    vec = pl.BlockSpec((None, 1, D), lambda i, j: (layer, 0, 0))
    gates = _resident((None, LRU_BLOCKS, LRU_BW, LRU_BW), lambda i, j: (layer, 0, 0, 0))
    return pl.pallas_call(
        _rg_kernel,
        grid=(b // bb, t // tt),
        in_specs=[
            pl.BlockSpec((bb, tt, D), lambda i, j: (i, j, 0)),
            pl.BlockSpec((bb, 1, D), lambda i, j: (i, 0, 0)),
            pl.BlockSpec((bb, CONV_W - 1, D), lambda i, j: (i, 0, 0)),
            vec,
            _resident((None, D, 2 * D), lambda i, j: (layer, 0, 0)),
            pl.BlockSpec((None, CONV_W, D), lambda i, j: (layer, 0, 0)),
            vec,
            gates, vec, gates, vec, vec,
            _resident((None, D, D), lambda i, j: (layer, 0, 0)),
        ],
        out_specs=[
            pl.BlockSpec((bb, tt, D), lambda i, j: (i, j, 0)),
            pl.BlockSpec((bb, 1, D), lambda i, j: (i, 0, 0)),
            pl.BlockSpec((bb, CONV_W - 1, D), lambda i, j: (i, 0, 0)),
        ],
        out_shape=[
            jax.ShapeDtypeStruct((b, t, D), F32),
            jax.ShapeDtypeStruct((b, 1, D), F32),
            jax.ShapeDtypeStruct((b, CONV_W - 1, D), F32),
        ],
        scratch_shapes=[
            pltpu.VMEM((bb, tt + V7X_SUBLANES, D), F32),
            pltpu.VMEM((bb, 1, D), F32),
            pltpu.VMEM((bb, tt, D), F32),
            pltpu.VMEM((bb, tt, D), F32),
            pltpu.VMEM((bb, tt, D), F32),
            pltpu.VMEM((2, rows, D), F32),
            pltpu.VMEM((D, 2 * D), BF16),
            pltpu.VMEM((LRU_BLOCKS, LRU_BW, LRU_BW), BF16),
            pltpu.VMEM((LRU_BLOCKS, LRU_BW, LRU_BW), BF16),
            pltpu.VMEM((D, D), BF16),
        ],
        compiler_params=_params(est, 2),
        name="rg_block",
    )(x3, h0, buf0, p["mix_norm"], p["rg_w_in"], p["rg_conv_w"], p["rg_conv_b"], p["rg_w_a"],
      p["rg_b_a"], p["rg_w_x"], p["rg_b_x"], p["rg_lambda"], p["rg_w_out"])


def _kv_kernel(x_ref, g_ref, w_ref, o_ref, w_b):
    @pl.when(pl.program_id(0) == 0)
    def _():
        w_b[...] = w_ref[...].astype(BF16)

    o_ref[...] = _bdot(_rms(x_ref[...], g_ref[...]).astype(BF16), w_b[...])


def _kv_proj(x2, norm, w, *, tm):
    t = x2.shape[0]
    n = w.shape[1]
    return pl.pallas_call(
        _kv_kernel,
        grid=(t // tm,),
        in_specs=[pl.BlockSpec((tm, D), lambda i: (i, 0)),
                  pl.BlockSpec((1, D), lambda i: (0, 0)),
                  _resident((D, n), lambda i: (0, 0))],
        out_specs=pl.BlockSpec((tm, n), lambda i: (i, 0)),
        out_shape=jax.ShapeDtypeStruct((t, n), F32),
        scratch_shapes=[pltpu.VMEM((D, n), BF16)],
        compiler_params=_params(6 * tm * D * 4, 1),
        name="kv_proj",
    )(x2, norm, w)


def _init_attn_tables(bias_ref, sinkc_ref, sink_ref, att_layer, r):
    shape = bias_ref.shape[1:]
    sh = int(math.log2(r))
    rho = lax.broadcasted_iota(jnp.int32, shape, 0)
    lane = lax.broadcasted_iota(jnp.int32, shape, 1)
    e = rho >> (sh + 2)
    m = (rho >> sh) & 3
    qi = rho & (r - 1)
    g = 2 * m + e
    diff = jnp.where(lane > qi, qi - lane + WINDOW, qi - lane)
    for kv in range(N_KV):
        head = (kv * GROUP + g + 1).astype(F32)
        slope = jnp.exp2(-0.5 * head)
        bias_ref[kv] = -slope * diff.astype(F32)
        gcol = g[:, :1]
        sc = jnp.zeros((shape[0], 1), F32)
        for gg in range(GROUP):
            sc = jnp.where(gcol == gg, sink_ref[att_layer, kv * GROUP + gg], sc)
        sinkc_ref[kv] = sc


def _attn_blocks(blocks, bias_ref, sinkc_ref, *, one_pass):
    half = HEAD_DIM
    low = lax.broadcasted_iota(jnp.int32, (1, V7X_LANES), 1) < half
    zero_b = jnp.zeros((), BF16)
    units = [(blk, kv) for blk in range(len(blocks)) for kv in range(N_KV)]
    swapped = {}

    def lane_swapped(blk):
        if blk not in swapped:
            _, k2, v2, _, _ = blocks[blk]
            swapped[blk] = (pltpu.roll(k2, half, 1), pltpu.roll(v2, half, 1))
        return swapped[blk]

    def scores(blk, kv):
        q, k2, _, _, _ = blocks[blk]
        k_sw, _ = lane_swapped(blk)
        kdup = jnp.where(low, k2, k_sw) if kv == 0 else jnp.where(low, k_sw, k2)
        parts = []
        for e in range(2):
            for m in range(GROUP // 2):
                c = kv * (GROUP // 2) + m
                qc = q[:, c * V7X_LANES:(c + 1) * V7X_LANES]
                parts.append(jnp.where(low, qc, 0.0) if e == 0 else jnp.where(low, 0.0, qc))
        qst = jnp.concatenate(parts, axis=0).astype(BF16)
        return lax.dot_general(qst, kdup, (((1,), (1,)), ((), ())), preferred_element_type=F32)

    def weighted_values(blk, kv, p, esink):
        q, _, v2, o_view, _ = blocks[blk]
        r = q.shape[0]
        _, v_sw = lane_swapped(blk)
        v_lo = jnp.where(low, v2 if kv == 0 else v_sw, zero_b)
        v_hi = jnp.where(low, zero_b, v_sw if kv == 0 else v2)
        ones = jnp.ones(v2.shape, BF16)
        hr = (GROUP // 2) * r
        o_even = _bdot(p[:hr], jnp.concatenate([v_lo, ones], axis=1))
        o_odd = _bdot(p[hr:], jnp.concatenate([v_hi, ones], axis=1))
        for m in range(GROUP // 2):
            c = kv * (GROUP // 2) + m
            rs = slice(m * r, (m + 1) * r)
            ne = o_even[rs, :V7X_LANES] / (o_even[rs, V7X_LANES:] + esink[m * r:(m + 1) * r])
            no = o_odd[rs, :V7X_LANES] / (o_odd[rs, V7X_LANES:] + esink[hr + m * r:hr + (m + 1) * r])
            o_view[:, c * V7X_LANES:(c + 1) * V7X_LANES] = ne + no

    r = blocks[0][0].shape[0]
    ur = GROUP * r

    def softmax_pass(group):
        n_rows = len(group) * ur
        rho = lax.broadcasted_iota(jnp.int32, (n_rows, V7X_LANES), 0)
        lane = lax.broadcasted_iota(jnp.int32, (n_rows, V7X_LANES), 1)
        prev = lane > (rho & (r - 1))
        s = jnp.concatenate([scores(*unit) for unit in group], axis=0)
        bias = jnp.concatenate([bias_ref[kv] for _, kv in group], axis=0)
        s = jnp.where(prev, s[:, :WINDOW], s[:, WINDOW:]) + bias
        if any(blocks[b][4] is not None for b, _ in group):
            zero_pen = jnp.zeros((), F32)
            pens = [jnp.full((ur, 1), zero_pen if blocks[b][4] is None else blocks[b][4], F32)
                    for b, _ in group]
            s = s + jnp.where(prev, jnp.concatenate(pens, axis=0), 0.0)
        sink = jnp.concatenate([sinkc_ref[kv] for _, kv in group], axis=0)
        mx = jnp.maximum(jnp.max(s, axis=1, keepdims=True), sink)
        p = jnp.exp(s - mx)
        esink = jnp.exp(sink - mx)
        p = jnp.concatenate([jnp.where(prev, p, 0.0), jnp.where(prev, 0.0, p)], axis=1).astype(BF16)
        for u, unit in enumerate(group):
            weighted_values(*unit, p[u * ur:(u + 1) * ur], esink[u * ur:(u + 1) * ur])

    if one_pass:
        softmax_pass(units)
    else:
        for unit in units:
            softmax_pass([unit])


def _attn_prompt_kernel(sink_ref, x_ref, kvc_ref, kvp_ref, g_ref, wq_ref, wo_ref, y_ref,
                        kvcat_ref, o_ref, bias_ref, sinkc_ref, wq_b, wo_b, *, att_layer):
    bi = pl.program_id(0)
    ti = pl.program_id(1)
    tq = x_ref.shape[0]

    @pl.when((bi == 0) & (ti == 0))
    def _():
        _init_attn_tables(bias_ref, sinkc_ref, sink_ref, att_layer, WINDOW)
        wq_b[...] = wq_ref[...].astype(BF16)
        wo_b[...] = wo_ref[...].astype(BF16)

    x = x_ref[...]
    hn = _rms(x, g_ref[...]).astype(BF16)
    q = _bdot(hn, wq_b[...]) * Q_SCALE
    kvcat_ref[0:WINDOW, :] = kvp_ref[...].astype(BF16)
    kvcat_ref[WINDOW:, :] = kvc_ref[...].astype(BF16)
    pen = jnp.where(ti == 0, NEG, 0.0).astype(F32)
    blocks = []
    for rb in range(tq // WINDOW):
        rows = slice(rb * WINDOW, rb * WINDOW + NK)
        blocks.append((q[rb * WINDOW:(rb + 1) * WINDOW],
                       kvcat_ref[rows, 0:V7X_LANES], kvcat_ref[rows, V7X_LANES:2 * V7X_LANES],
                       o_ref.at[rb * WINDOW:(rb + 1) * WINDOW, :], pen if rb == 0 else None))
    _attn_blocks(blocks, bias_ref, sinkc_ref, one_pass=False)
    y_ref[...] = x + _bdot(o_ref[...].astype(BF16), wo_b[...])


def _attn_prompt(x3, kv3, p, att_layer, *, tq):
    b, t, _ = x3.shape
    sub = tq // WINDOW
    est = (4 * tq * D * 4 + 2 * D * D * (4 + 2) + 2 * (tq + WINDOW) * 2 * V7X_LANES * 4
           + N_KV * GROUP * WINDOW * 2 * V7X_LANES * 4 + 10 * tq * D * 4)
    layer = N_REC + att_layer
    return pl.pallas_call(
        functools.partial(_attn_prompt_kernel, att_layer=att_layer),
        grid=(b, t // tq),
        in_specs=[
            pl.BlockSpec(memory_space=pltpu.SMEM),
            pl.BlockSpec((None, tq, D), lambda i, j: (i, j, 0)),
            pl.BlockSpec((None, tq, 2 * V7X_LANES), lambda i, j: (i, j, 0)),
            pl.BlockSpec((None, WINDOW, 2 * V7X_LANES),
                         lambda i, j: (i, jnp.maximum(j * sub - 1, 0), 0)),
            pl.BlockSpec((None, 1, D), lambda i, j: (layer, 0, 0)),
            _resident((None, D, D), lambda i, j: (att_layer, 0, 0)),
            _resident((None, D, D), lambda i, j: (att_layer, 0, 0)),
        ],
        out_specs=pl.BlockSpec((None, tq, D), lambda i, j: (i, j, 0)),
        out_shape=jax.ShapeDtypeStruct((b, t, D), F32),
        scratch_shapes=[
            pltpu.VMEM((tq + WINDOW, 2 * V7X_LANES), BF16),
            pltpu.VMEM((tq, D), F32),
            pltpu.VMEM((N_KV, GROUP * WINDOW, V7X_LANES), F32),
            pltpu.VMEM((N_KV, GROUP * WINDOW, 1), F32),
            pltpu.VMEM((D, D), BF16),
            pltpu.VMEM((D, D), BF16),
        ],
        compiler_params=_params(est, 2),
        name="attn_prompt",
    )(p["attn_sinks"], x3, kv3, kv3, p["mix_norm"], p["attn_w_q"], p["attn_w_o"])


def _attn_sample_kernel(sink_ref, x_ref, k_ref, v_ref, g_ref, wq_ref, wo_ref, y_ref,
                        q_ref, o_ref, bias_ref, sinkc_ref, wq_b, wo_b, *, att_layer):
    bb, tq, _ = x_ref.shape

    @pl.when(pl.program_id(0) == 0)
    def _():
        _init_attn_tables(bias_ref, sinkc_ref, sink_ref, att_layer, tq)
        wq_b[...] = wq_ref[...].astype(BF16)
        wo_b[...] = wo_ref[...].astype(BF16)

    x = x_ref[...].reshape(bb * tq, D)
    hn = _rms(x, g_ref[...]).astype(BF16)
    q_ref[...] = (_bdot(hn, wq_b[...]) * Q_SCALE).reshape(bb, tq, D)

    def some_sequences(g, carry):
        blocks = []
        for d in range(SAMPLE_SEQS_PER_ITER):
            i = g * SAMPLE_SEQS_PER_ITER + d
            blocks.append((q_ref[i], k_ref[i], v_ref[i], o_ref.at[i], None))
        _attn_blocks(blocks, bias_ref, sinkc_ref, one_pass=True)
        return carry

    lax.fori_loop(0, bb // SAMPLE_SEQS_PER_ITER, some_sequences, 0)
    y = x + _bdot(o_ref[...].reshape(bb * tq, D).astype(BF16), wo_b[...])
    y_ref[...] = y.reshape(bb, tq, D)


def _attn_sample(x3, k_all, v_all, p, att_layer, *, bb):
    b, tq, _ = x3.shape
    est = 2 * D * D * (4 + 2) + 12 * bb * tq * D * 4 + 4 * bb * NK * V7X_LANES * 2
    layer = N_REC + att_layer
    return pl.pallas_call(
        functools.partial(_attn_sample_kernel, att_layer=att_layer),
        grid=(b // bb,),
        in_specs=[
            pl.BlockSpec(memory_space=pltpu.SMEM),
            pl.BlockSpec((bb, tq, D), lambda i: (i, 0, 0)),
            pl.BlockSpec((bb, NK, V7X_LANES), lambda i: (i, 0, 0)),
            pl.BlockSpec((bb, NK, V7X_LANES), lambda i: (i, 0, 0)),
            pl.BlockSpec((None, 1, D), lambda i: (layer, 0, 0)),
            _resident((None, D, D), lambda i: (att_layer, 0, 0)),
            _resident((None, D, D), lambda i: (att_layer, 0, 0)),
        ],
        out_specs=pl.BlockSpec((bb, tq, D), lambda i: (i, 0, 0)),
        out_shape=jax.ShapeDtypeStruct((b, tq, D), F32),
        scratch_shapes=[
            pltpu.VMEM((bb, tq, D), F32),
            pltpu.VMEM((bb, tq, D), F32),
            pltpu.VMEM((N_KV, GROUP * tq, V7X_LANES), F32),
            pltpu.VMEM((N_KV, GROUP * tq, 1), F32),
            pltpu.VMEM((D, D), BF16),
            pltpu.VMEM((D, D), BF16),
        ],
        compiler_params=_params(est, 1),
        name="attn_sample",
    )(p["attn_sinks"], x3, k_all, v_all, p["mix_norm"], p["attn_w_q"], p["attn_w_o"])


def _prepare_params(ffn1_norm, ffn1_w_gu, ffn1_w_down, mix_norm, ffn2_norm, ffn2_w_gu, ffn2_w_down,
                    rg_w_in, rg_conv_w, rg_conv_b, rg_w_a, rg_b_a, rg_w_x, rg_b_x, rg_lambda,
                    rg_w_out, kv_norm, w_kv, attn_w_q, attn_sinks, attn_w_o, final_norm):
    def vec(v):
        return v.reshape(v.shape[0], 1, D)

    return dict(
        ffn1_norm=vec(ffn1_norm), ffn1_w_gu=_cast_gu(ffn1_w_gu), ffn1_w_down=_cast_down(ffn1_w_down),
        ffn2_norm=vec(ffn2_norm), ffn2_w_gu=_cast_gu(ffn2_w_gu), ffn2_w_down=_cast_down(ffn2_w_down),
        mix_norm=vec(mix_norm),
        rg_w_in=rg_w_in, rg_conv_w=rg_conv_w, rg_conv_b=vec(rg_conv_b),
        rg_w_a=rg_w_a, rg_b_a=vec(rg_b_a), rg_w_x=rg_w_x,
        rg_b_x=vec(rg_b_x), rg_lambda=vec(rg_lambda), rg_w_out=rg_w_out,
        kv_norm=kv_norm.reshape(1, D), w_kv=w_kv,
        attn_w_q=attn_w_q, attn_sinks=attn_sinks, attn_w_o=attn_w_o,
        final_norm=final_norm.reshape(1, D),
    )


def _trunk(x, rg_h0, rg_buf0, cache_k, cache_v, p, *, ffn_tm, rg_bb, rg_tt, att_tile):
    b, t, _ = x.shape
    is_prompt = cache_k is None
    rg_h, rg_buf = [], []
    k_att = v_att = kv3 = new_k = new_v = None
    for l in range(N_LAYERS):
        x = _ffn(x.reshape(b * t, D), p["ffn1_norm"], p["ffn1_w_gu"], p["ffn1_w_down"],
                 p["final_norm"], l, final_norm=False, tm=ffn_tm).reshape(b, t, D)
        if l < N_REC:
            x, hl, bl = _rg_block(x, rg_h0[l].reshape(b, 1, D), rg_buf0[l], p, l, bb=rg_bb, tt=rg_tt)
            rg_h.append(hl.reshape(b, D))
            rg_buf.append(bl)
        elif is_prompt:
            x = _attn_prompt(x, kv3, p, l - N_REC, tq=att_tile)
        else:
            x = _attn_sample(x, k_att, v_att, p, l - N_REC, bb=att_tile)
        x = _ffn(x.reshape(b * t, D), p["ffn2_norm"], p["ffn2_w_gu"], p["ffn2_w_down"],
                 p["final_norm"], l, final_norm=(l == N_LAYERS - 1), tm=ffn_tm).reshape(b, t, D)
        if l == N_REC - 1:
            kv = _kv_proj(x.reshape(b * t, D), p["kv_norm"], p["w_kv"], tm=ffn_tm)
            kv3 = kv.reshape(b, t, 2 * N_KV * HEAD_DIM)
            k_new, v_new = kv3[..., :N_KV * HEAD_DIM], kv3[..., N_KV * HEAD_DIM:]
            if is_prompt:
                wb = min(WINDOW, t)
                new_k, new_v = k_new[:, t - wb:], v_new[:, t - wb:]
            else:
                wb = cache_k.shape[1]
                ck = cache_k.reshape(b, wb, N_KV * HEAD_DIM)
                cv = cache_v.reshape(b, wb, N_KV * HEAD_DIM)
                k_cat = jnp.concatenate([ck, k_new], axis=1)
                v_cat = jnp.concatenate([cv, v_new], axis=1)
                new_k, new_v = k_cat[:, t:], v_cat[:, t:]
                pad = ((0, 0), (0, NK - wb - t), (0, 0))
                k_att = jnp.pad(k_cat.astype(BF16), pad)
                v_att = jnp.pad(v_cat.astype(BF16), pad)
            new_k = new_k.reshape(b, wb, N_KV, HEAD_DIM)
            new_v = new_v.reshape(b, wb, N_KV, HEAD_DIM)
    return x, jnp.stack(rg_h), jnp.stack(rg_buf), new_k, new_v


def kernel(x_prompt, x_sample, state_rg_h, state_rg_conv, cache_k, cache_v, ffn1_norm, ffn1_w_gu, ffn1_w_down, mix_norm, ffn2_norm, ffn2_w_gu, ffn2_w_down, rg_w_in, rg_conv_w, rg_conv_b, rg_w_a, rg_b_a, rg_w_x, rg_b_x, rg_lambda, rg_w_out, kv_norm, w_kv, attn_w_q, attn_sinks, attn_w_o, final_norm):
    p = _prepare_params(ffn1_norm, ffn1_w_gu, ffn1_w_down, mix_norm, ffn2_norm, ffn2_w_gu,
                        ffn2_w_down, rg_w_in, rg_conv_w, rg_conv_b, rg_w_a, rg_b_a, rg_w_x, rg_b_x,
                        rg_lambda, rg_w_out, kv_norm, w_kv, attn_w_q, attn_sinks, attn_w_o,
                        final_norm)
    b = x_prompt.shape[0]
    h0 = jnp.zeros((N_REC, b, D), x_prompt.dtype)
    buf0 = jnp.zeros((N_REC, b, CONV_W - 1, D), x_prompt.dtype)
    y_p, p_h, p_conv, p_k, p_v = _trunk(x_prompt, h0, buf0, None, None, p,
                                        ffn_tm=1024, rg_bb=1, rg_tt=512, att_tile=512)
    y_s, s_h, s_conv, s_k, s_v = _trunk(x_sample, state_rg_h, state_rg_conv, cache_k, cache_v, p,
                                        ffn_tm=1024, rg_bb=32, rg_tt=8, att_tile=32)
    return (y_p, y_s, p_h, p_conv, p_k, p_v, s_h, s_conv, s_k, s_v)
```

```python
import functools
import math

import jax
import jax.numpy as jnp
from jax import lax
from jax.experimental import pallas as pl
from jax.experimental.pallas import tpu as pltpu

D = 1024
D_FF = 2816
N_LAYERS = 4
N_REC = 2
LRU_BLOCKS = 4
LRU_BW = D // LRU_BLOCKS
CONV_W = 4
LRU_C = 8.0
HEAD_DIM = 64
N_HEADS = D // HEAD_DIM
N_KV = 2
GROUP = N_HEADS // N_KV
WINDOW = 128
EPS = 1e-6
NEG = -1e30
Q_SCALE = HEAD_DIM ** -0.5

V7X_LANES = 128
V7X_SUBLANES = 8
V7X_VMEM_BYTES = 64 * 1024 * 1024

FF_CHUNK = 256
N_FF_CHUNKS = D_FF // FF_CHUNK
NK = 2 * WINDOW
SAMPLE_SEQS_PER_ITER = 16

BF16 = jnp.bfloat16
F32 = jnp.float32


def _vmem_limit(estimate_bytes):
    return int(min(max(estimate_bytes, 16 * 1024 * 1024), V7X_VMEM_BYTES - 6 * 1024 * 1024))


def _params(estimate_bytes, n_axes):
    return pltpu.CompilerParams(dimension_semantics=("arbitrary",) * n_axes,
                                vmem_limit_bytes=_vmem_limit(estimate_bytes))


def _resident(block_shape, index_map):
    return pl.BlockSpec(block_shape, index_map, pipeline_mode=pl.Buffered(1))


def _rms(x, g):
    ms = jnp.mean(x * x, axis=-1, keepdims=True)
    return x * lax.rsqrt(ms + EPS) * g


def _bdot(a, b):
    return jnp.dot(a, b, preferred_element_type=F32)


def _cast_gu_kernel(w_ref, o_ref):
    for j in range(N_FF_CHUNKS):
        gate = slice(j * FF_CHUNK, (j + 1) * FF_CHUNK)
        up = slice(D_FF + j * FF_CHUNK, D_FF + (j + 1) * FF_CHUNK)
        o_ref[j, :, :FF_CHUNK] = w_ref[:, gate].astype(BF16)
        o_ref[j, :, FF_CHUNK:] = w_ref[:, up].astype(BF16)


def _cast_gu(w):
    rows = D // 4
    return pl.pallas_call(
        _cast_gu_kernel,
        grid=(N_LAYERS, D // rows),
        in_specs=[pl.BlockSpec((None, rows, 2 * D_FF), lambda l, r: (l, r, 0))],
        out_specs=pl.BlockSpec((None, N_FF_CHUNKS, rows, 2 * FF_CHUNK), lambda l, r: (l, 0, r, 0)),
        out_shape=jax.ShapeDtypeStruct((N_LAYERS, N_FF_CHUNKS, D, 2 * FF_CHUNK), BF16),
        compiler_params=_params(2 * rows * 2 * D_FF * (4 + 2) + 2 * rows * 2 * D_FF * 4, 2),
        name="cast_gu",
    )(w)


def _cast_kernel(w_ref, o_ref):
    o_ref[...] = w_ref[...].astype(BF16)


def _cast_down(w):
    rows = D_FF // 2
    out = pl.pallas_call(
        _cast_kernel,
        grid=(N_LAYERS, D_FF // rows),
        in_specs=[pl.BlockSpec((None, rows, D), lambda l, j: (l, j, 0))],
        out_specs=pl.BlockSpec((None, rows, D), lambda l, j: (l, j, 0)),
        out_shape=jax.ShapeDtypeStruct((N_LAYERS, D_FF, D), BF16),
        compiler_params=_params(6 * rows * D * 4, 2),
        name="cast_down",
    )(w)
    return out.reshape(N_LAYERS, N_FF_CHUNKS, FF_CHUNK, D)


def _ffn_kernel(x_ref, g_ref, wgu_ref, wd_ref, fg_ref, kvg_ref, wkv_ref, *rest, final_norm, with_kv):
    if with_kv:
        o_ref, kv_ref, n_ref, h_ref, wkv_b = rest
    else:
        o_ref, n_ref, h_ref = rest
    n_ref[...] = _rms(x_ref[...], g_ref[...]).astype(BF16)
    for j in range(N_FF_CHUNKS):
        gu = _bdot(n_ref[...], wgu_ref[j])
        gate, up = gu[:, :FF_CHUNK], gu[:, FF_CHUNK:]
        h_ref[j] = (gate * jax.nn.sigmoid(gate) * up).astype(BF16)
    acc = _bdot(h_ref[0], wd_ref[0])
    for j in range(1, N_FF_CHUNKS):
        acc = acc + _bdot(h_ref[j], wd_ref[j])
    y = x_ref[...] + 0.5 * acc
    if final_norm:
        y = _rms(y, fg_ref[...])
    o_ref[...] = y
    if with_kv:
        @pl.when(pl.program_id(0) == 0)
        def _():
            wkv_b[...] = wkv_ref[...].astype(BF16)

        kv_ref[...] = _bdot(_rms(y, kvg_ref[...]).astype(BF16), wkv_b[...])


def _ffn(x2, norm, wgu, wd, fgain, kv_norm, w_kv, layer, *, final_norm, with_kv, tm):
    t = x2.shape[0]
    n_kv = w_kv.shape[1]
    est = (4 * tm * D * 4 + tm * D * 2 + tm * D_FF * 2 + 3 * D * D_FF * 2
           + 3 * tm * 2 * FF_CHUNK * 4 + 2 * tm * D * 4 + D * n_kv * 6 + 2 * tm * n_kv * 4)
    x_spec = pl.BlockSpec((tm, D), lambda i: (i, 0))
    out_specs, out_shape = x_spec, jax.ShapeDtypeStruct((t, D), F32)
    scratch = [pltpu.VMEM((tm, D), BF16), pltpu.VMEM((N_FF_CHUNKS, tm, FF_CHUNK), BF16)]
    if with_kv:
        out_specs = [x_spec, pl.BlockSpec((tm, n_kv), lambda i: (i, 0))]
        out_shape = [out_shape, jax.ShapeDtypeStruct((t, n_kv), F32)]
        scratch.append(pltpu.VMEM((D, n_kv), BF16))
    return pl.pallas_call(
        functools.partial(_ffn_kernel, final_norm=final_norm, with_kv=with_kv),
        grid=(t // tm,),
        in_specs=[
            x_spec,
            pl.BlockSpec((None, 1, D), lambda i: (layer, 0, 0)),
            _resident((None, N_FF_CHUNKS, D, 2 * FF_CHUNK), lambda i: (layer, 0, 0, 0)),
            _resident((None, N_FF_CHUNKS, FF_CHUNK, D), lambda i: (layer, 0, 0, 0)),
            pl.BlockSpec((1, D), lambda i: (0, 0)),
            pl.BlockSpec((1, D), lambda i: (0, 0)),
            _resident((D, n_kv), lambda i: (0, 0)),
        ],
        out_specs=out_specs,
        out_shape=out_shape,
        scratch_shapes=scratch,
        compiler_params=_params(est, 1),
        name="ffn",
    )(x2, norm, wgu, wd, fgain, kv_norm, w_kv)


def _softplus(z):
    return jnp.maximum(z, 0.0) + jnp.log1p(jnp.exp(-jnp.abs(z)))


def _gelu_tanh(z):
    return 0.5 * z * (1.0 + jnp.tanh(math.sqrt(2.0 / math.pi) * (z + 0.044715 * (z * z * z))))


def _rg_kernel(x_ref, h0_ref, buf0_ref, g_ref, win_ref, cw_ref, cb_ref, wa_ref, ba_ref, wx_ref,
               bx_ref, lam_ref, wout_ref, y_ref, hl_ref, bufo_ref,
               ext_ref, hc_ref, a_ref, u_ref, hs_ref, gpre_ref,
               win_b, wa_b, wx_b, wout_b):
    t = pl.program_id(1)
    bb, tt, _ = x_ref.shape
    rows = bb * tt
    hist = V7X_SUBLANES

    @pl.when((pl.program_id(0) == 0) & (t == 0))
    def _():
        win_b[...] = win_ref[...].astype(BF16)
        wa_b[...] = wa_ref[...].astype(BF16)
        wx_b[...] = wx_ref[...].astype(BF16)
        wout_b[...] = wout_ref[...].astype(BF16)

    @pl.when(t == 0)
    def _():
        ext_ref[:, hist - (CONV_W - 1):hist, :] = buf0_ref[...]
        hc_ref[...] = h0_ref[...]

    x = x_ref[...].reshape(rows, D)
    hn = _rms(x, g_ref[...]).astype(BF16)
    proj = _bdot(hn, win_b[...])
    gate, xr = proj[:, :D], proj[:, D:]
    ext_ref[:, hist:hist + tt, :] = xr.reshape(bb, tt, D)

    xc = cb_ref[...].reshape(1, 1, D)
    for k in range(CONV_W):
        lo = hist - (CONV_W - 1) + k
        xc = xc + ext_ref[:, lo:lo + tt, :] * cw_ref[k:k + 1, :].reshape(1, 1, D)
    xc = xc.reshape(rows, D)

    xcb = xc.astype(BF16)
    for n in range(LRU_BLOCKS):
        sl = slice(n * LRU_BW, (n + 1) * LRU_BW)
        gpre_ref[0, :, sl] = _bdot(xcb[:, sl], wa_b[n])
        gpre_ref[1, :, sl] = _bdot(xcb[:, sl], wx_b[n])
    r = jax.nn.sigmoid(gpre_ref[0] + ba_ref[...])
    i = jax.nn.sigmoid(gpre_ref[1] + bx_ref[...])
    log_a = (-LRU_C) * r * _softplus(-lam_ref[...])
    a = jnp.exp(log_a)
    u = jnp.sqrt(-jnp.tanh(log_a) * (a * a + 1.0)) * (i * xc)

    a3 = a.reshape(rows // V7X_SUBLANES, V7X_SUBLANES, D)
    u3 = u.reshape(rows // V7X_SUBLANES, V7X_SUBLANES, D)
    tin = lax.broadcasted_iota(jnp.int32, a3.shape, 1)
    for s in (1, 2, 4):
        a_prev = pltpu.roll(a3, s, 1)
        u_prev = pltpu.roll(u3, s, 1)
        keep = tin >= s
        u3 = jnp.where(keep, a3 * u_prev + u3, u3)
        a3 = jnp.where(keep, a3 * a_prev, a3)
    a_ref[...] = a3.reshape(bb, tt, D)
    u_ref[...] = u3.reshape(bb, tt, D)

    carry = hc_ref[...]
    for gi in range(tt // V7X_SUBLANES):
        sl = slice(gi * V7X_SUBLANES, (gi + 1) * V7X_SUBLANES)
        h = a_ref[:, sl, :] * carry + u_ref[:, sl, :]
        hs_ref[:, sl, :] = h
        carry = h[:, V7X_SUBLANES - 1:, :]
    hc_ref[...] = carry

    hseq = hs_ref[...].reshape(rows, D)
    y = _bdot((hseq * _gelu_tanh(gate)).astype(BF16), wout_b[...])
    y_ref[...] = (x + y).reshape(bb, tt, D)

    hl_ref[...] = carry
    bufo_ref[...] = ext_ref[:, hist + tt - (CONV_W - 1):hist + tt, :]
    ext_ref[:, 0:hist, :] = ext_ref[:, tt:tt + hist, :]


def _rg_block(x3, h0, buf0, p, layer, *, bb, tt):
    b, t, _ = x3.shape
    rows = bb * tt
    n_w = 2 * D * D + D * D + 2 * LRU_BLOCKS * LRU_BW * LRU_BW
    est = 4 * rows * D * 4 + n_w * (4 + 2) + 16 * rows * D * 4
    vec = pl.BlockSpec((None, 1, D), lambda i, j: (layer, 0, 0))
    gates = _resident((None, LRU_BLOCKS, LRU_BW, LRU_BW), lambda i, j: (layer, 0, 0, 0))
    return pl.pallas_call(
        _rg_kernel,
        grid=(b // bb, t // tt),
        in_specs=[
            pl.BlockSpec((bb, tt, D), lambda i, j: (i, j, 0)),
            pl.BlockSpec((bb, 1, D), lambda i, j: (i, 0, 0)),
            pl.BlockSpec((bb, CONV_W - 1, D), lambda i, j: (i, 0, 0)),
            vec,
            _resident((None, D, 2 * D), lambda i, j: (layer, 0, 0)),
            pl.BlockSpec((None, CONV_W, D), lambda i, j: (layer, 0, 0)),
            vec,
            gates, vec, gates, vec, vec,
            _resident((None, D, D), lambda i, j: (layer, 0, 0)),
        ],
        out_specs=[
            pl.BlockSpec((bb, tt, D), lambda i, j: (i, j, 0)),
            pl.BlockSpec((bb, 1, D), lambda i, j: (i, 0, 0)),
            pl.BlockSpec((bb, CONV_W - 1, D), lambda i, j: (i, 0, 0)),
        ],
        out_shape=[
            jax.ShapeDtypeStruct((b, t, D), F32),
            jax.ShapeDtypeStruct((b, 1, D), F32),
            jax.ShapeDtypeStruct((b, CONV_W - 1, D), F32),
        ],
        scratch_shapes=[
            pltpu.VMEM((bb, tt + V7X_SUBLANES, D), F32),
            pltpu.VMEM((bb, 1, D), F32),
            pltpu.VMEM((bb, tt, D), F32),
            pltpu.VMEM((bb, tt, D), F32),
            pltpu.VMEM((bb, tt, D), F32),
            pltpu.VMEM((2, rows, D), F32),
            pltpu.VMEM((D, 2 * D), BF16),
            pltpu.VMEM((LRU_BLOCKS, LRU_BW, LRU_BW), BF16),
            pltpu.VMEM((LRU_BLOCKS, LRU_BW, LRU_BW), BF16),
            pltpu.VMEM((D, D), BF16),
        ],
        compiler_params=_params(est, 2),
        name="rg_block",
    )(x3, h0, buf0, p["mix_norm"], p["rg_w_in"], p["rg_conv_w"], p["rg_conv_b"], p["rg_w_a"],
      p["rg_b_a"], p["rg_w_x"], p["rg_b_x"], p["rg_lambda"], p["rg_w_out"])


def _init_attn_tables(bias_ref, sinkc_ref, sink_ref, att_layer, r):
    shape = bias_ref.shape[1:]
    sh = int(math.log2(r))
    rho = lax.broadcasted_iota(jnp.int32, shape, 0)
    lane = lax.broadcasted_iota(jnp.int32, shape, 1)
    e = rho >> (sh + 2)
    m = (rho >> sh) & 3
    qi = rho & (r - 1)
    g = 2 * m + e
    diff = jnp.where(lane > qi, qi - lane + WINDOW, qi - lane)
    for kv in range(N_KV):
        head = (kv * GROUP + g + 1).astype(F32)
        slope = jnp.exp2(-0.5 * head)
        bias_ref[kv] = -slope * diff.astype(F32)
        gcol = g[:, :1]
        sc = jnp.zeros((shape[0], 1), F32)
        for gg in range(GROUP):
            sc = jnp.where(gcol == gg, sink_ref[att_layer, kv * GROUP + gg], sc)
        sinkc_ref[kv] = sc


def _attn_blocks(blocks, bias_ref, sinkc_ref, *, one_pass):
    half = HEAD_DIM
    low = lax.broadcasted_iota(jnp.int32, (1, V7X_LANES), 1) < half
    zero_b = jnp.zeros((), BF16)
    units = [(blk, kv) for blk in range(len(blocks)) for kv in range(N_KV)]
    swapped = {}

    def lane_swapped(blk):
        if blk not in swapped:
            _, k2, v2, _, _ = blocks[blk]
            swapped[blk] = (pltpu.roll(k2, half, 1), pltpu.roll(v2, half, 1))
        return swapped[blk]

    def scores(blk, kv):
        q, k2, _, _, _ = blocks[blk]
        k_sw, _ = lane_swapped(blk)
        kdup = jnp.where(low, k2, k_sw) if kv == 0 else jnp.where(low, k_sw, k2)
        parts = []
        for e in range(2):
            for m in range(GROUP // 2):
                c = kv * (GROUP // 2) + m
                qc = q[:, c * V7X_LANES:(c + 1) * V7X_LANES]
                parts.append(jnp.where(low, qc, 0.0) if e == 0 else jnp.where(low, 0.0, qc))
        qst = jnp.concatenate(parts, axis=0).astype(BF16)
        return lax.dot_general(qst, kdup, (((1,), (1,)), ((), ())), preferred_element_type=F32)

    def weighted_values(blk, kv, p, esink):
        q, _, v2, o_view, _ = blocks[blk]
        r = q.shape[0]
        _, v_sw = lane_swapped(blk)
        v_lo = jnp.where(low, v2 if kv == 0 else v_sw, zero_b)
        v_hi = jnp.where(low, zero_b, v_sw if kv == 0 else v2)
        ones = jnp.ones(v2.shape, BF16)
        hr = (GROUP // 2) * r
        o_even = _bdot(p[:hr], jnp.concatenate([v_lo, ones], axis=1))
        o_odd = _bdot(p[hr:], jnp.concatenate([v_hi, ones], axis=1))
        for m in range(GROUP // 2):
            c = kv * (GROUP // 2) + m
            rs = slice(m * r, (m + 1) * r)
            ne = o_even[rs, :V7X_LANES] / (o_even[rs, V7X_LANES:] + esink[m * r:(m + 1) * r])
            no = o_odd[rs, :V7X_LANES] / (o_odd[rs, V7X_LANES:] + esink[hr + m * r:hr + (m + 1) * r])
            o_view[:, c * V7X_LANES:(c + 1) * V7X_LANES] = ne + no

    r = blocks[0][0].shape[0]
    ur = GROUP * r

    def softmax_pass(group):
        n_rows = len(group) * ur
        rho = lax.broadcasted_iota(jnp.int32, (n_rows, V7X_LANES), 0)
        lane = lax.broadcasted_iota(jnp.int32, (n_rows, V7X_LANES), 1)
        prev = lane > (rho & (r - 1))
        s = jnp.concatenate([scores(*unit) for unit in group], axis=0)
        bias = jnp.concatenate([bias_ref[kv] for _, kv in group], axis=0)
        s = jnp.where(prev, s[:, :WINDOW], s[:, WINDOW:]) + bias
        if any(blocks[b][4] is not None for b, _ in group):
            zero_pen = jnp.zeros((), F32)
            pens = [jnp.full((ur, 1), zero_pen if blocks[b][4] is None else blocks[b][4], F32)
                    for b, _ in group]
            s = s + jnp.where(prev, jnp.concatenate(pens, axis=0), 0.0)
        sink = jnp.concatenate([sinkc_ref[kv] for _, kv in group], axis=0)
        mx = jnp.maximum(jnp.max(s, axis=1, keepdims=True), sink)
        p = jnp.exp(s - mx)
        esink = jnp.exp(sink - mx)
        p = jnp.concatenate([jnp.where(prev, p, 0.0), jnp.where(prev, 0.0, p)], axis=1).astype(BF16)
        for u, unit in enumerate(group):
            weighted_values(*unit, p[u * ur:(u + 1) * ur], esink[u * ur:(u + 1) * ur])

    if one_pass:
        softmax_pass(units)
    else:
        for unit in units:
            softmax_pass([unit])


def _attn_prompt_kernel(sink_ref, x_ref, kvc_ref, kvp_ref, g_ref, wq_ref, wo_ref, y_ref,
                        kvcat_ref, o_ref, bias_ref, sinkc_ref, wq_b, wo_b, *, att_layer):
    bi = pl.program_id(0)
    ti = pl.program_id(1)
    tq = x_ref.shape[0]

    @pl.when((bi == 0) & (ti == 0))
    def _():
        _init_attn_tables(bias_ref, sinkc_ref, sink_ref, att_layer, WINDOW)
        wq_b[...] = wq_ref[...].astype(BF16)
        wo_b[...] = wo_ref[...].astype(BF16)

    x = x_ref[...]
    hn = _rms(x, g_ref[...]).astype(BF16)
    q = _bdot(hn, wq_b[...]) * Q_SCALE
    kvcat_ref[0:WINDOW, :] = kvp_ref[...].astype(BF16)
    kvcat_ref[WINDOW:, :] = kvc_ref[...].astype(BF16)
    pen = jnp.where(ti == 0, NEG, 0.0).astype(F32)
    blocks = []
    for rb in range(tq // WINDOW):
        rows = slice(rb * WINDOW, rb * WINDOW + NK)
        blocks.append((q[rb * WINDOW:(rb + 1) * WINDOW],
                       kvcat_ref[rows, 0:V7X_LANES], kvcat_ref[rows, V7X_LANES:2 * V7X_LANES],
                       o_ref.at[rb * WINDOW:(rb + 1) * WINDOW, :], pen if rb == 0 else None))
    _attn_blocks(blocks, bias_ref, sinkc_ref, one_pass=False)
    y_ref[...] = x + _bdot(o_ref[...].astype(BF16), wo_b[...])


def _attn_prompt(x3, kv3, p, att_layer, *, tq):
    b, t, _ = x3.shape
    sub = tq // WINDOW
    est = (4 * tq * D * 4 + 2 * D * D * (4 + 2) + 2 * (tq + WINDOW) * 2 * V7X_LANES * 4
           + N_KV * GROUP * WINDOW * 2 * V7X_LANES * 4 + 10 * tq * D * 4)
    layer = N_REC + att_layer
    return pl.pallas_call(
        functools.partial(_attn_prompt_kernel, att_layer=att_layer),
        grid=(b, t // tq),
        in_specs=[
            pl.BlockSpec(memory_space=pltpu.SMEM),
            pl.BlockSpec((None, tq, D), lambda i, j: (i, j, 0)),
            pl.BlockSpec((None, tq, 2 * V7X_LANES), lambda i, j: (i, j, 0)),
            pl.BlockSpec((None, WINDOW, 2 * V7X_LANES),
                         lambda i, j: (i, jnp.maximum(j * sub - 1, 0), 0)),
            pl.BlockSpec((None, 1, D), lambda i, j: (layer, 0, 0)),
            _resident((None, D, D), lambda i, j: (att_layer, 0, 0)),
            _resident((None, D, D), lambda i, j: (att_layer, 0, 0)),
        ],
        out_specs=pl.BlockSpec((None, tq, D), lambda i, j: (i, j, 0)),
        out_shape=jax.ShapeDtypeStruct((b, t, D), F32),
        scratch_shapes=[
            pltpu.VMEM((tq + WINDOW, 2 * V7X_LANES), BF16),
            pltpu.VMEM((tq, D), F32),
            pltpu.VMEM((N_KV, GROUP * WINDOW, V7X_LANES), F32),
            pltpu.VMEM((N_KV, GROUP * WINDOW, 1), F32),
            pltpu.VMEM((D, D), BF16),
            pltpu.VMEM((D, D), BF16),
        ],
        compiler_params=_params(est, 2),
        name="attn_prompt",
    )(p["attn_sinks"], x3, kv3, kv3, p["mix_norm"], p["attn_w_q"], p["attn_w_o"])


def _attn_sample_kernel(sink_ref, x_ref, k_ref, v_ref, g_ref, wq_ref, wo_ref, y_ref,
                        q_ref, o_ref, bias_ref, sinkc_ref, wq_b, wo_b, *, att_layer):
    bb, tq, _ = x_ref.shape

    @pl.when(pl.program_id(0) == 0)
    def _():
        _init_attn_tables(bias_ref, sinkc_ref, sink_ref, att_layer, tq)
        wq_b[...] = wq_ref[...].astype(BF16)
        wo_b[...] = wo_ref[...].astype(BF16)

    x = x_ref[...].reshape(bb * tq, D)
    hn = _rms(x, g_ref[...]).astype(BF16)
    q_ref[...] = (_bdot(hn, wq_b[...]) * Q_SCALE).reshape(bb, tq, D)

    def some_sequences(g, carry):
        blocks = []
        for d in range(SAMPLE_SEQS_PER_ITER):
            i = g * SAMPLE_SEQS_PER_ITER + d
            blocks.append((q_ref[i], k_ref[i], v_ref[i], o_ref.at[i], None))
        _attn_blocks(blocks, bias_ref, sinkc_ref, one_pass=True)
        return carry

    lax.fori_loop(0, bb // SAMPLE_SEQS_PER_ITER, some_sequences, 0)
    y = x + _bdot(o_ref[...].reshape(bb * tq, D).astype(BF16), wo_b[...])
    y_ref[...] = y.reshape(bb, tq, D)


def _attn_sample(x3, k_all, v_all, p, att_layer, *, bb):
    b, tq, _ = x3.shape
    est = 2 * D * D * (4 + 2) + 12 * bb * tq * D * 4 + 4 * bb * NK * V7X_LANES * 2
    layer = N_REC + att_layer
    return pl.pallas_call(
        functools.partial(_attn_sample_kernel, att_layer=att_layer),
        grid=(b // bb,),
        in_specs=[
            pl.BlockSpec(memory_space=pltpu.SMEM),
            pl.BlockSpec((bb, tq, D), lambda i: (i, 0, 0)),
            pl.BlockSpec((bb, NK, V7X_LANES), lambda i: (i, 0, 0)),
            pl.BlockSpec((bb, NK, V7X_LANES), lambda i: (i, 0, 0)),
            pl.BlockSpec((None, 1, D), lambda i: (layer, 0, 0)),
            _resident((None, D, D), lambda i: (att_layer, 0, 0)),
            _resident((None, D, D), lambda i: (att_layer, 0, 0)),
        ],
        out_specs=pl.BlockSpec((bb, tq, D), lambda i: (i, 0, 0)),
        out_shape=jax.ShapeDtypeStruct((b, tq, D), F32),
        scratch_shapes=[
            pltpu.VMEM((bb, tq, D), F32),
            pltpu.VMEM((bb, tq, D), F32),
            pltpu.VMEM((N_KV, GROUP * tq, V7X_LANES), F32),
            pltpu.VMEM((N_KV, GROUP * tq, 1), F32),
            pltpu.VMEM((D, D), BF16),
            pltpu.VMEM((D, D), BF16),
        ],
        compiler_params=_params(est, 1),
        name="attn_sample",
    )(p["attn_sinks"], x3, k_all, v_all, p["mix_norm"], p["attn_w_q"], p["attn_w_o"])


def _prepare_params(ffn1_norm, ffn1_w_gu, ffn1_w_down, mix_norm, ffn2_norm, ffn2_w_gu, ffn2_w_down,
                    rg_w_in, rg_conv_w, rg_conv_b, rg_w_a, rg_b_a, rg_w_x, rg_b_x, rg_lambda,
                    rg_w_out, kv_norm, w_kv, attn_w_q, attn_sinks, attn_w_o, final_norm):
    def vec(v):
        return v.reshape(v.shape[0], 1, D)

    return dict(
        ffn1_norm=vec(ffn1_norm), ffn1_w_gu=_cast_gu(ffn1_w_gu), ffn1_w_down=_cast_down(ffn1_w_down),
        ffn2_norm=vec(ffn2_norm), ffn2_w_gu=_cast_gu(ffn2_w_gu), ffn2_w_down=_cast_down(ffn2_w_down),
        mix_norm=vec(mix_norm),
        rg_w_in=rg_w_in, rg_conv_w=rg_conv_w, rg_conv_b=vec(rg_conv_b),
        rg_w_a=rg_w_a, rg_b_a=vec(rg_b_a), rg_w_x=rg_w_x,
        rg_b_x=vec(rg_b_x), rg_lambda=vec(rg_lambda), rg_w_out=rg_w_out,
        kv_norm=kv_norm.reshape(1, D), w_kv=w_kv,
        attn_w_q=attn_w_q, attn_sinks=attn_sinks, attn_w_o=attn_w_o,
        final_norm=final_norm.reshape(1, D),
    )


def _trunk(x, rg_h0, rg_buf0, cache_k, cache_v, p, *, ffn_tm, rg_bb, rg_tt, att_tile):
    b, t, _ = x.shape
    is_prompt = cache_k is None
    rg_h, rg_buf = [], []
    k_att = v_att = kv3 = new_k = new_v = None
    for l in range(N_LAYERS):
        x = _ffn(x.reshape(b * t, D), p["ffn1_norm"], p["ffn1_w_gu"], p["ffn1_w_down"],
                 p["final_norm"], p["kv_norm"], p["w_kv"], l, final_norm=False, with_kv=False,
                 tm=ffn_tm).reshape(b, t, D)
        if l < N_REC:
            x, hl, bl = _rg_block(x, rg_h0[l].reshape(b, 1, D), rg_buf0[l], p, l, bb=rg_bb, tt=rg_tt)
            rg_h.append(hl.reshape(b, D))
            rg_buf.append(bl)
        elif is_prompt:
            x = _attn_prompt(x, kv3, p, l - N_REC, tq=att_tile)
        else:
            x = _attn_sample(x, k_att, v_att, p, l - N_REC, bb=att_tile)
        with_kv = l == N_REC - 1
        x = _ffn(x.reshape(b * t, D), p["ffn2_norm"], p["ffn2_w_gu"], p["ffn2_w_down"],
                 p["final_norm"], p["kv_norm"], p["w_kv"], l, final_norm=(l == N_LAYERS - 1),
                 with_kv=with_kv, tm=ffn_tm)
        if with_kv:
            x, kv = x
        x = x.reshape(b, t, D)
        if with_kv:
            kv3 = kv.reshape(b, t, 2 * N_KV * HEAD_DIM)
            k_new, v_new = kv3[..., :N_KV * HEAD_DIM], kv3[..., N_KV * HEAD_DIM:]
            if is_prompt:
                wb = min(WINDOW, t)
                new_k, new_v = k_new[:, t - wb:], v_new[:, t - wb:]
            else:
                wb = cache_k.shape[1]
                ck = cache_k.reshape(b, wb, N_KV * HEAD_DIM)
                cv = cache_v.reshape(b, wb, N_KV * HEAD_DIM)
                k_cat = jnp.concatenate([ck, k_new], axis=1)
                v_cat = jnp.concatenate([cv, v_new], axis=1)
                new_k, new_v = k_cat[:, t:], v_cat[:, t:]
                pad = ((0, 0), (0, NK - wb - t), (0, 0))
                k_att = jnp.pad(k_cat.astype(BF16), pad)
                v_att = jnp.pad(v_cat.astype(BF16), pad)
            new_k = new_k.reshape(b, wb, N_KV, HEAD_DIM)
            new_v = new_v.reshape(b, wb, N_KV, HEAD_DIM)
    return x, jnp.stack(rg_h), jnp.stack(rg_buf), new_k, new_v


def kernel(x_prompt, x_sample, state_rg_h, state_rg_conv, cache_k, cache_v, ffn1_norm, ffn1_w_gu, ffn1_w_down, mix_norm, ffn2_norm, ffn2_w_gu, ffn2_w_down, rg_w_in, rg_conv_w, rg_conv_b, rg_w_a, rg_b_a, rg_w_x, rg_b_x, rg_lambda, rg_w_out, kv_norm, w_kv, attn_w_q, attn_sinks, attn_w_o, final_norm):
    p = _prepare_params(ffn1_norm, ffn1_w_gu, ffn1_w_down, mix_norm, ffn2_norm, ffn2_w_gu,
                        ffn2_w_down, rg_w_in, rg_conv_w, rg_conv_b, rg_w_a, rg_b_a, rg_w_x, rg_b_x,
                        rg_lambda, rg_w_out, kv_norm, w_kv, attn_w_q, attn_sinks, attn_w_o,
                        final_norm)
    b = x_prompt.shape[0]
    h0 = jnp.zeros((N_REC, b, D), x_prompt.dtype)
    buf0 = jnp.zeros((N_REC, b, CONV_W - 1, D), x_prompt.dtype)
    y_p, p_h, p_conv, p_k, p_v = _trunk(x_prompt, h0, buf0, None, None, p,
                                        ffn_tm=1024, rg_bb=1, rg_tt=512, att_tile=512)
    y_s, s_h, s_conv, s_k, s_v = _trunk(x_sample, state_rg_h, state_rg_conv, cache_k, cache_v, p,
                                        ffn_tm=1024, rg_bb=32, rg_tt=8, att_tile=32)
    return (y_p, y_s, p_h, p_conv, p_k, p_v, s_h, s_conv, s_k, s_v)
```
